```python
import math
import jax, jax.numpy as jnp
from jax import lax
import numpy as np

D_MODEL = 4096
BATCH = 4
SEQ = 2048
DEPTH = 2
DEC_BATCH = 128
DEC_SEQ = 1
PAST_LEN = 16384
PAGE_SIZE = 128

HEAD_DIM = 128
N_HEADS_TOTAL = D_MODEL // HEAD_DIM
DN_HEADS = (3 * N_HEADS_TOTAL) // 8
GLA_HEADS = (N_HEADS_TOTAL - DN_HEADS) // 2
HG_HEADS = N_HEADS_TOTAL - DN_HEADS - GLA_HEADS
DN_DIM = DN_HEADS * HEAD_DIM
GLA_DIM = GLA_HEADS * HEAD_DIM
HG_DIM = HG_HEADS * HEAD_DIM
D_MIX = DN_DIM + GLA_DIM + HG_DIM
DN_CONV = 4
GLA_LOWRANK = 16
GLA_NORMALIZER = 16.0
D_FF = 256 * ((8 * D_MODEL // 3 + 255) // 256)
FFN_CONV = 3
CHUNK = 64
EPS = 1e-6
LB_FLOOR = 1e-30
PROJ_SIZES = (3 * DN_DIM, DN_HEADS, DN_HEADS, DN_DIM,
              GLA_DIM, GLA_DIM, GLA_DIM, GLA_LOWRANK, GLA_DIM,
              HG_DIM, HG_DIM, HG_DIM, HG_DIM)
D_IN = 4 * DN_DIM + 2 * DN_HEADS + 4 * GLA_DIM + GLA_LOWRANK + 4 * HG_DIM

kernel_name = 'hybrid_deltanet_gla_hgrn2_convffn_step'


def _rmsnorm(x, w):
    xf = x.astype(jnp.float32)
    xf = xf * lax.rsqrt(jnp.mean(jnp.square(xf), axis=-1, keepdims=True) + EPS)
    return xf.astype(x.dtype) * w


def _l2norm(x):
    return x * lax.rsqrt(jnp.sum(x * x, axis=-1, keepdims=True) + EPS)


def _split_proj(proj):
    out, start = [], 0
    for s in PROJ_SIZES:
        out.append(proj[..., start:start + s])
        start += s
    return out


def _heads(x, n):
    B, T, _ = x.shape
    return x.reshape(B, T, n, -1).transpose(0, 2, 1, 3)


def _pad_time(a, pad):
    widths = [(0, 0)] * a.ndim
    widths[2] = (0, pad)
    return jnp.pad(a, widths)


def _to_chunks(a, C):
    B, H, T = a.shape[:3]
    return jnp.moveaxis(a.reshape((B, H, T // C, C) + a.shape[3:]), 2, 0)


def _from_chunks(a):
    N, B, H, C, d = a.shape
    return jnp.moveaxis(a, 0, 2).reshape(B, H, N * C, d)


def _masked_exp(diff, mask):
    return jnp.where(mask, jnp.exp(jnp.where(mask, diff, 0.0)), 0.0)


def _causal_dwconv(x, prev, w):
    W = w.shape[0]
    T = x.shape[1]
    xp = jnp.concatenate([prev.astype(x.dtype), x], axis=1)
    y = sum(xp[:, i:i + T] * w[i] for i in range(W))
    return y, xp[:, xp.shape[1] - (W - 1):]


def _chunked_gla(q, k, v, log_a, s0):
    f32 = jnp.float32
    T = q.shape[2]
    C = min(CHUNK, T)
    pad = (-T) % C
    qc, kc, vc, gc = [_to_chunks(_pad_time(a.astype(f32), pad), C) for a in (q, k, v, log_a)]
    incl = jnp.tril(jnp.ones((C, C), dtype=bool))[:, :, None]

    def step(S, inp):
        qi, ki, vi, gi = inp
        G = jnp.cumsum(gi, axis=2)
        dec = _masked_exp(G[:, :, :, None, :] - G[:, :, None, :, :], incl)
        A = jnp.einsum('bhid,bhjd,bhijd->bhij', qi, ki, dec)
        o = (jnp.einsum('bhid,bhdv->bhiv', qi * jnp.exp(G), S)
             + jnp.einsum('bhij,bhjv->bhiv', A, vi))
        G_end = G[:, :, -1:, :]
        S = (S * jnp.exp(G_end)[:, :, 0, :, None]
             + jnp.einsum('bhjd,bhjv->bhdv', ki * jnp.exp(G_end - G), vi))
        return S, o

    S, o = lax.scan(step, s0.astype(f32), (qc, kc, vc, gc))
    return _from_chunks(o)[:, :, :T], S


def _chunked_gated_delta(q, k, v, beta, g, s0):
    f32 = jnp.float32
    T = q.shape[2]
    dv = v.shape[-1]
    C = min(CHUNK, T)
    pad = (-T) % C
    qc, kc, vc, bc, gc = [_to_chunks(_pad_time(a.astype(f32), pad), C) for a in (q, k, v, beta, g)]
    incl = jnp.tril(jnp.ones((C, C), dtype=bool))
    strict = jnp.tril(jnp.ones((C, C), dtype=bool), k=-1)
    eye = jnp.eye(C, dtype=f32)

    def step(S, inp):
        qi, ki, vi, bi, gi = inp
        G = jnp.cumsum(gi, axis=-1)
        gamma = _masked_exp(G[..., :, None] - G[..., None, :], incl)
        kk = jnp.einsum('bhid,bhjd->bhij', ki, ki)
        M = jnp.where(strict, bi[..., :, None] * kk * gamma, 0.0)
        rhs = jnp.concatenate([vi * bi[..., None], ki * (bi * jnp.exp(G))[..., None]], axis=-1)
        sol = lax.linalg.triangular_solve(eye + M, rhs, left_side=True, lower=True, unit_diagonal=True)
        u, w = sol[..., :dv], sol[..., dv:]
        v_new = u - jnp.einsum('bhid,bhdv->bhiv', w, S)
        attn = jnp.einsum('bhid,bhjd->bhij', qi, ki) * gamma
        o = (jnp.einsum('bhid,bhdv->bhiv', qi * jnp.exp(G)[..., None], S)
             + jnp.einsum('bhij,bhjv->bhiv', attn, v_new))
        G_end = G[..., -1:]
        S = (S * jnp.exp(G_end)[..., None]
             + jnp.einsum('bhjd,bhjv->bhdv', ki * jnp.exp(G_end - G)[..., None], v_new))
        return S, o

    S, o = lax.scan(step, s0.astype(f32), (qc, kc, vc, bc, gc))
    return _from_chunks(o)[:, :, :T], S


def _gated_head_norm(o, gate, gain):
    B, H, T, d = o.shape
    o = o.transpose(0, 2, 1, 3)
    o = o * lax.rsqrt(jnp.mean(jnp.square(o), axis=-1, keepdims=True) + EPS) * gain.astype(jnp.float32)
    return o.reshape(B, T, H * d) * jax.nn.silu(gate.astype(jnp.float32))


def _mixer(h, conv_prev, s_dn, s_gla, s_hg, w_in, dn_conv_w, dn_a_log, dn_dt_bias, dn_norm,
           gla_w_gate, gla_gate_bias, gla_norm, hg_lb, hg_norm, w_out):
    f32 = jnp.float32
    (dn_qkv, dn_b, dn_a, dn_g, gla_q, gla_k, gla_v, gla_lr, gla_g,
     hg_q, hg_f, hg_i, hg_g) = _split_proj(h @ w_in)

    qkv, conv_new = _causal_dwconv(dn_qkv, conv_prev, dn_conv_w)
    q, k, v = jnp.split(jax.nn.silu(qkv.astype(f32)), 3, axis=-1)
    q = _l2norm(_heads(q, DN_HEADS)) * HEAD_DIM ** -0.5
    k = _l2norm(_heads(k, DN_HEADS))
    v = _heads(v, DN_HEADS)
    beta = jax.nn.sigmoid(dn_b.astype(f32)).transpose(0, 2, 1)
    g = (-jnp.exp(dn_a_log.astype(f32))
         * jax.nn.softplus(dn_a.astype(f32) + dn_dt_bias.astype(f32))).transpose(0, 2, 1)
    o_dn, s_dn_new = _chunked_gated_delta(q, k, v, beta, g, s_dn)
    o_dn = _gated_head_norm(o_dn, dn_g, dn_norm)

    gq = _heads(gla_q.astype(f32), GLA_HEADS) * HEAD_DIM ** -0.5
    gk = _heads(gla_k.astype(f32), GLA_HEADS)
    gv = _heads(gla_v.astype(f32), GLA_HEADS)
    ga = _heads(jax.nn.log_sigmoid((gla_lr @ gla_w_gate + gla_gate_bias).astype(f32)) / GLA_NORMALIZER,
                GLA_HEADS)
    o_gla, s_gla_new = _chunked_gla(gq, gk, gv, ga, s_gla)
    o_gla = _gated_head_norm(o_gla, gla_g, gla_norm)

    lb = hg_lb.astype(f32)
    zf = hg_f.astype(f32)
    log_f = jnp.logaddexp(jnp.log(jnp.maximum(lb, LB_FLOOR)), jnp.log1p(-lb) + jax.nn.log_sigmoid(zf))
    hk = (1.0 - lb) * jax.nn.sigmoid(-zf)
    hq = jax.nn.silu(hg_q.astype(f32))
    o_hg, s_hg_new = _chunked_gla(_heads(hq, HG_HEADS), _heads(hk, HG_HEADS),
                                  _heads(hg_i.astype(f32), HG_HEADS), _heads(log_f, HG_HEADS), s_hg)
    o_hg = _gated_head_norm(o_hg, hg_g, hg_norm)

    o = jnp.concatenate([o_dn, o_gla, o_hg], axis=-1).astype(h.dtype)
    return (o @ w_out, conv_new, s_dn_new.astype(s_dn.dtype),
            s_gla_new.astype(s_gla.dtype), s_hg_new.astype(s_hg.dtype))


def _conv_ffn(h, prev, w_gate, w_up, conv_w, conv_b, w_down):
    gate, new_prev = _causal_dwconv(h @ w_gate, prev, conv_w)
    return (jax.nn.silu(gate + conv_b) * (h @ w_up)) @ w_down, new_prev


def _trunk(x, st_dn_conv, st_dn, st_gla, st_hg, st_ffn, hg_lbs, weights):
    (norm_mix, w_in, dn_conv_w, dn_a_log, dn_dt_bias, dn_norm, gla_w_gate, gla_gate_bias, gla_norm,
     hg_norm, w_out, norm_ffn, ffn_w_gate, ffn_w_up, ffn_conv_w, ffn_conv_b, ffn_w_down, norm_final) = weights
    outs = ([], [], [], [], [])
    for l in range(DEPTH):
        m, c_dn, s_dn, s_gla, s_hg = _mixer(
            _rmsnorm(x, norm_mix[l]), st_dn_conv[l], st_dn[l], st_gla[l], st_hg[l], w_in[l],
            dn_conv_w[l], dn_a_log[l], dn_dt_bias[l], dn_norm[l], gla_w_gate[l], gla_gate_bias[l],
            gla_norm[l], hg_lbs[l], hg_norm[l], w_out[l])
        x = x + m
        f, c_ffn = _conv_ffn(_rmsnorm(x, norm_ffn[l]), st_ffn[l], ffn_w_gate[l], ffn_w_up[l],
                             ffn_conv_w[l], ffn_conv_b[l], ffn_w_down[l])
        x = x + f
        for lst, val in zip(outs, (c_dn, s_dn, s_gla, s_hg, c_ffn)):
            lst.append(val)
    y = _rmsnorm(x, norm_final)
    return (y,) + tuple(jnp.stack(lst, axis=0) for lst in outs)


def setup_inputs(seed: int = 0) -> dict:
    key = jax.random.key(seed)
    ks = jax.random.split(key, 32)
    f32 = jnp.float32
    L = DEPTH

    def nrm(k, shape, s):
        return s * jax.random.normal(k, shape, f32)

    dt = jnp.exp(jax.random.uniform(ks[9], (L, DN_HEADS), f32, math.log(1e-3), math.log(1e-1)))
    return {
        'x_prompt': nrm(ks[0], (BATCH, SEQ, D_MODEL), 1.0),
        'x_sample': nrm(ks[1], (DEC_BATCH, DEC_SEQ, D_MODEL), 1.0),
        'state_dn_conv': nrm(ks[2], (L, DEC_BATCH, DN_CONV - 1, 3 * DN_DIM), 1.0),
        'state_dn': nrm(ks[3], (L, DEC_BATCH, DN_HEADS, HEAD_DIM, HEAD_DIM), 0.1),
        'state_gla': nrm(ks[4], (L, DEC_BATCH, GLA_HEADS, HEAD_DIM, HEAD_DIM), 0.3),
        'state_hg': nrm(ks[5], (L, DEC_BATCH, HG_HEADS, HEAD_DIM, HEAD_DIM), 0.3),
        'state_ffn_conv': nrm(ks[6], (L, DEC_BATCH, FFN_CONV - 1, D_FF), 1.0),
        'norm_mix': 1.0 + nrm(ks[7], (L, D_MODEL), 0.02),
        'w_in': nrm(ks[8], (L, D_MODEL, D_IN), D_MODEL ** -0.5),
        'dn_conv_w': nrm(ks[10], (L, DN_CONV, 3 * DN_DIM), DN_CONV ** -0.5),
        'dn_a_log': jnp.log(jax.random.uniform(ks[11], (L, DN_HEADS), f32, 1.0, 16.0)),
        'dn_dt_bias': dt + jnp.log(-jnp.expm1(-dt)),
        'dn_norm': 1.0 + nrm(ks[12], (L, HEAD_DIM), 0.02),
        'gla_w_gate': nrm(ks[13], (L, GLA_LOWRANK, GLA_DIM), GLA_LOWRANK ** -0.5),
        'gla_gate_bias': nrm(ks[14], (L, GLA_DIM), 0.1),
        'gla_norm': 1.0 + nrm(ks[15], (L, HEAD_DIM), 0.02),
        'hg_lb_logits': nrm(ks[16], (L, HG_DIM), 0.5),
        'hg_norm': 1.0 + nrm(ks[17], (L, HEAD_DIM), 0.02),
        'w_out': nrm(ks[18], (L, D_MIX, D_MODEL), D_MIX ** -0.5),
        'norm_ffn': 1.0 + nrm(ks[19], (L, D_MODEL), 0.02),
        'ffn_w_gate': nrm(ks[20], (L, D_MODEL, D_FF), D_MODEL ** -0.5),
        'ffn_w_up': nrm(ks[21], (L, D_MODEL, D_FF), D_MODEL ** -0.5),
        'ffn_conv_w': nrm(ks[22], (L, FFN_CONV, D_FF), FFN_CONV ** -0.5),
        'ffn_conv_b': nrm(ks[23], (L, D_FF), 0.02),
        'ffn_w_down': nrm(ks[24], (L, D_FF, D_MODEL), D_FF ** -0.5),
        'norm_final': 1.0 + nrm(ks[25], (D_MODEL,), 0.02),
    }


def reference(x_prompt, x_sample, state_dn_conv, state_dn, state_gla, state_hg, state_ffn_conv,
              norm_mix, w_in, dn_conv_w, dn_a_log, dn_dt_bias, dn_norm, gla_w_gate, gla_gate_bias,
              gla_norm, hg_lb_logits, hg_norm, w_out, norm_ffn, ffn_w_gate, ffn_w_up, ffn_conv_w,
              ffn_conv_b, ffn_w_down, norm_final):
    p = jax.nn.softmax(hg_lb_logits.astype(jnp.float32), axis=0)
    hg_lbs = jnp.cumsum(p, axis=0) - p[0:1]
    weights = (norm_mix, w_in, dn_conv_w, dn_a_log, dn_dt_bias, dn_norm, gla_w_gate, gla_gate_bias,
               gla_norm, hg_norm, w_out, norm_ffn, ffn_w_gate, ffn_w_up, ffn_conv_w, ffn_conv_b,
               ffn_w_down, norm_final)

    dt = x_prompt.dtype
    zp_dn_conv = jnp.zeros((DEPTH, BATCH, DN_CONV - 1, 3 * DN_DIM), dt)
    zp_dn = jnp.zeros((DEPTH, BATCH, DN_HEADS, HEAD_DIM, HEAD_DIM), dt)
    zp_gla = jnp.zeros((DEPTH, BATCH, GLA_HEADS, HEAD_DIM, HEAD_DIM), dt)
    zp_hg = jnp.zeros((DEPTH, BATCH, HG_HEADS, HEAD_DIM, HEAD_DIM), dt)
    zp_ffn = jnp.zeros((DEPTH, BATCH, FFN_CONV - 1, D_FF), dt)

    y_prompt, p_dn_conv, p_dn, p_gla, p_hg, p_ffn_conv = _trunk(
        x_prompt, zp_dn_conv, zp_dn, zp_gla, zp_hg, zp_ffn, hg_lbs, weights)
    y_sample, s_dn_conv, s_dn, s_gla, s_hg, s_ffn_conv = _trunk(
        x_sample, state_dn_conv, state_dn, state_gla, state_hg, state_ffn_conv, hg_lbs, weights)
    return (y_prompt, y_sample, p_dn_conv, p_dn, p_gla, p_hg, p_ffn_conv,
            s_dn_conv, s_dn, s_gla, s_hg, s_ffn_conv)
```

```python
import functools
import math

import jax
import jax.numpy as jnp
from jax import lax
from jax.experimental import pallas as pl
from jax.experimental.pallas import tpu as pltpu

F32 = jnp.float32
BF16 = jnp.bfloat16
HIGHEST = lax.Precision.HIGHEST

HEAD_DIM = 128
DN_CONV = 4
GLA_LOWRANK = 16
GLA_NORMALIZER = 16.0
FFN_CONV = 3
EPS = 1e-6
LB_FLOOR = 1e-30

LANES = 128
SUBLANES = 8
MXU_DIM = 256
VMEM_LIMIT_BYTES = 56 * 1024 * 1024

CHUNK = 64
SUB = 16
FF_ALIGN = 1024


def _cparams(sem):
    return pltpu.CompilerParams(dimension_semantics=sem, vmem_limit_bytes=VMEM_LIMIT_BYTES)


def _sigmoid(x):
    return 1.0 / (1.0 + jnp.exp(-x))


def _silu(x):
    return x * _sigmoid(x)


def _softplus(x):
    return jnp.maximum(x, 0.0) + jnp.log1p(jnp.exp(-jnp.abs(x)))


def _log_sigmoid(x):
    return -_softplus(-x)


def _logaddexp(a, b):
    return jnp.maximum(a, b) + jnp.log1p(jnp.exp(-jnp.abs(a - b)))


def _dot(a, b):
    return jnp.dot(a, b, preferred_element_type=F32)


def _dot_nt(a, b, precision=None):
    return lax.dot_general(a, b, (((1,), (1,)), ((), ())), preferred_element_type=F32, precision=precision)


def _dot_tn(a, b):
    return lax.dot_general(a, b, (((0,), (0,)), ((), ())), preferred_element_type=F32)


def _tri_incl(n):
    r = lax.broadcasted_iota(jnp.int32, (n, n), 0)
    c = lax.broadcasted_iota(jnp.int32, (n, n), 1)
    return r, c


def _gated_head_norm(o, gate, gain):
    o = o * lax.rsqrt(jnp.mean(o * o, axis=-1, keepdims=True) + EPS) * gain
    return o * _silu(gate)


def _rmsnorm_kernel(x_ref, w_ref, o_ref):
    x = x_ref[...]
    y = x * lax.rsqrt(jnp.mean(x * x, axis=-1, keepdims=True) + EPS)
    o_ref[...] = (y * w_ref[...]).astype(o_ref.dtype)


def _rmsnorm(x, w, out_dtype):
    m, d = x.shape
    tm = min(m, 256)
    return pl.pallas_call(
        _rmsnorm_kernel,
        out_shape=jax.ShapeDtypeStruct((m, d), out_dtype),
        grid=(m // tm,),
        in_specs=[pl.BlockSpec((tm, d), lambda i: (i, 0)), pl.BlockSpec((1, d), lambda i: (0, 0))],
        out_specs=pl.BlockSpec((tm, d), lambda i: (i, 0)),
        compiler_params=_cparams(("parallel",)),
        name="rmsnorm",
    )(x, w.reshape(1, d))


def _mm_fullk_kernel(*refs, n_lhs, has_res, slab):
    lhs = refs[:n_lhs]
    rhs = refs[n_lhs:2 * n_lhs]
    res_ref = refs[2 * n_lhs] if has_res else None
    o_ref = refs[2 * n_lhs + (1 if has_res else 0)]
    acc = _dot(lhs[0][...], rhs[0][...])
    for a, b in zip(lhs[1:], rhs[1:]):
        acc = acc + _dot(a[...], b[...])
    if has_res:
        acc = res_ref[...] + acc
    if slab:
        for j in range(o_ref.shape[0]):
            o_ref[j] = acc[:, j * LANES:(j + 1) * LANES]
    else:
        o_ref[...] = acc.astype(o_ref.dtype)


def _matmul(lhs_list, rhs_list, *, res=None, slab=False, tm, tn, name):
    m = lhs_list[0].shape[0]
    n = rhs_list[0].shape[1]
    tm = min(tm, m)
    tn = min(tn, n)
    in_specs = [pl.BlockSpec((tm, a.shape[1]), lambda i, j: (i, 0)) for a in lhs_list]
    in_specs += [pl.BlockSpec((b.shape[0], tn), lambda i, j: (0, j)) for b in rhs_list]
    args = list(lhs_list) + list(rhs_list)
    if res is not None:
        in_specs.append(pl.BlockSpec((tm, tn), lambda i, j: (i, j)))
        args.append(res)
    if slab:
        out_shape = jax.ShapeDtypeStruct((n // LANES, m, LANES), F32)
        out_spec = pl.BlockSpec((tn // LANES, tm, LANES), lambda i, j: (j, i, 0))
    else:
        out_shape = jax.ShapeDtypeStruct((m, n), F32)
        out_spec = pl.BlockSpec((tm, tn), lambda i, j: (i, j))
    return pl.pallas_call(
        functools.partial(_mm_fullk_kernel, n_lhs=len(lhs_list), has_res=res is not None, slab=slab),
        out_shape=out_shape,
        grid=(m // tm, n // tn),
        in_specs=in_specs,
        out_specs=out_spec,
        compiler_params=_cparams(("parallel", "arbitrary")),
        name=name,
    )(*args)


def _mm_ktiled_kernel(x_ref, w_ref, res_ref, o_ref, acc_ref):
    k = pl.program_id(2)

    @pl.when(k == 0)
    def _():
        acc_ref[...] = res_ref[...]

    acc_ref[...] += _dot(x_ref[...], w_ref[...])

    @pl.when(k == pl.num_programs(2) - 1)
    def _():
        o_ref[...] = acc_ref[...]


def _matmul_ktiled(x, w, res, *, tm, tn, tk, name):
    m, kdim = x.shape
    n = w.shape[1]
    tm = min(tm, m)
    tn = min(tn, n)
    return pl.pallas_call(
        _mm_ktiled_kernel,
        out_shape=jax.ShapeDtypeStruct((m, n), F32),
        grid=(m // tm, n // tn, kdim // tk),
        in_specs=[
            pl.BlockSpec((tm, tk), lambda i, j, k: (i, k)),
            pl.BlockSpec((tk, tn), lambda i, j, k: (k, j)),
            pl.BlockSpec((tm, tn), lambda i, j, k: (i, j)),
        ],
        out_specs=pl.BlockSpec((tm, tn), lambda i, j, k: (i, j)),
        scratch_shapes=[pltpu.VMEM((tm, tn), F32)],
        compiler_params=_cparams(("parallel", "parallel", "arbitrary")),
        name=name,
    )(x, w, res)


def _ffn_up_prompt_kernel(x_ref, wg_ref, wu_ref, cw_ref, cb_ref, hid_ref, tail_ref, carry_ref, *, tiles_per_seq):
    m = pl.program_id(1)
    x = x_ref[...]
    gate = _dot(x, wg_ref[...])
    up = _dot(x, wu_ref[...])
    tm = gate.shape[0]

    @pl.when((m % tiles_per_seq) == 0)
    def _():
        carry_ref[...] = jnp.zeros_like(carry_ref)

    cat = jnp.concatenate([carry_ref[...], gate], axis=0)
    cw = cw_ref[...]
    y = cw[2:3] * gate
    for i in range(FFN_CONV - 1):
        back = FFN_CONV - 1 - i
        y = y + cw[i:i + 1] * cat[SUBLANES - back:SUBLANES - back + tm]
    hid_ref[...] = (_silu(y + cb_ref[...]) * up).astype(hid_ref.dtype)
    last = gate[tm - SUBLANES:]
    carry_ref[...] = last
    tail_ref[...] = last


def _ffn_up_prompt(h, wg, wu, cw, cb, *, batch, tm, tn):
    m, d = h.shape
    n = wg.shape[1]
    seq = m // batch
    tm = min(tm, seq)
    tiles_per_seq = seq // tm
    return pl.pallas_call(
        functools.partial(_ffn_up_prompt_kernel, tiles_per_seq=tiles_per_seq),
        out_shape=(jax.ShapeDtypeStruct((m, n), BF16), jax.ShapeDtypeStruct((batch, SUBLANES, n), F32)),
        grid=(n // tn, m // tm),
        in_specs=[
            pl.BlockSpec((tm, d), lambda j, i: (i, 0)),
            pl.BlockSpec((d, tn), lambda j, i: (0, j)),
            pl.BlockSpec((d, tn), lambda j, i: (0, j)),
            pl.BlockSpec((FFN_CONV, tn), lambda j, i: (0, j)),
            pl.BlockSpec((1, tn), lambda j, i: (0, j)),
        ],
        out_specs=(
            pl.BlockSpec((tm, tn), lambda j, i: (i, j)),
            pl.BlockSpec((None, SUBLANES, tn), lambda j, i: (i // tiles_per_seq, 0, j)),
        ),
        scratch_shapes=[pltpu.VMEM((SUBLANES, tn), F32)],
        compiler_params=_cparams(("parallel", "arbitrary")),
        name="ffn_up_prompt",
    )(h, wg, wu, cw, cb)


def _ffn_up_decode_kernel(x_ref, wg_ref, wu_ref, s0_ref, s1_ref, cw_ref, cb_ref, hid_ref, gate_ref):
    x = x_ref[...]
    gate = _dot(x, wg_ref[...])
    up = _dot(x, wu_ref[...])
    cw = cw_ref[...]
    y = cw[0:1] * s0_ref[...] + cw[1:2] * s1_ref[...] + cw[2:3] * gate
    hid_ref[...] = (_silu(y + cb_ref[...]) * up).astype(hid_ref.dtype)
    gate_ref[...] = gate


def _ffn_up_decode(h, wg, wu, state2d, cw, cb, *, tn):
    m, d = h.shape
    n = wg.shape[1]
    nt = n // tn
    return pl.pallas_call(
        _ffn_up_decode_kernel,
        out_shape=(jax.ShapeDtypeStruct((m, n), BF16), jax.ShapeDtypeStruct((m, n), F32)),
        grid=(nt,),
        in_specs=[
            pl.BlockSpec((m, d), lambda j: (0, 0)),
            pl.BlockSpec((d, tn), lambda j: (0, j)),
            pl.BlockSpec((d, tn), lambda j: (0, j)),
            pl.BlockSpec((m, tn), lambda j: (0, j)),
            pl.BlockSpec((m, tn), lambda j: (0, nt + j)),
            pl.BlockSpec((FFN_CONV, tn), lambda j: (0, j)),
            pl.BlockSpec((1, tn), lambda j: (0, j)),
        ],
        out_specs=(pl.BlockSpec((m, tn), lambda j: (0, j)), pl.BlockSpec((m, tn), lambda j: (0, j))),
        compiler_params=_cparams(("parallel",)),
        name="ffn_up_decode",
    )(h, wg, wu, state2d, state2d, cw, cb)


def _lane_select(x, lane_index):
    lane = lax.broadcasted_iota(jnp.int32, x.shape, 1)
    return jnp.sum(jnp.where(lane == lane_index, x, 0.0), axis=1, keepdims=True)


def _causal_conv_chunk(x_ref, cw, c, rows):
    start = pl.multiple_of(c * rows, rows)
    xc = x_ref[pl.ds(start, rows), :]
    pstart = pl.multiple_of(jnp.maximum(c * rows - SUBLANES, 0), SUBLANES)
    xp = x_ref[pl.ds(pstart, SUBLANES), :]
    xp = jnp.where(c == 0, 0.0, xp)
    cat = jnp.concatenate([xp, xc], axis=0)
    y = cw[DN_CONV - 1:DN_CONV] * xc
    for i in range(DN_CONV - 1):
        back = DN_CONV - 1 - i
        y = y + cw[i:i + 1] * cat[SUBLANES - back:SUBLANES - back + rows]
    return y


def _l2norm(x):
    return x * lax.rsqrt(jnp.sum(x * x, axis=-1, keepdims=True) + EPS)


def _unit_lower_inverse(mm):
    n = mm.shape[0]
    r, c = _tri_incl(n)
    eye = (r == c).astype(F32)
    x = -mm
    p = eye + x
    steps = int(math.log2(n)) - 1
    for _ in range(steps):
        x = jnp.dot(x, x, preferred_element_type=F32, precision=HIGHEST)
        p = p + jnp.dot(p, x, preferred_element_type=F32, precision=HIGHEST)
    return p


def _dn_prompt_kernel(q_ref, k_ref, v_ref, gt_ref, sm_ref, cwq_ref, cwk_ref, cwv_ref, alog_ref, dtb_ref, gain_ref,
                      o_ref, s_ref, u_s, w_s, qe_s, kd_s, attn_s, gend_s, st_s, *, n_heads):
    h = pl.program_id(1)
    seq = q_ref.shape[0]
    n_chunks = seq // CHUNK
    r, c = _tri_incl(CHUNK)
    tri = (r >= c).astype(F32)
    incl = r >= c
    strict = r > c
    cwq = cwq_ref[...]
    cwk = cwk_ref[...]
    cwv = cwv_ref[...]
    neg_a = -jnp.exp(alog_ref[...])
    dtb = dtb_ref[...]

    def phase1(ci, carry):
        rows = pl.ds(pl.multiple_of(ci * CHUNK, CHUNK), CHUNK)
        q = _silu(_causal_conv_chunk(q_ref, cwq, ci, CHUNK))
        k = _silu(_causal_conv_chunk(k_ref, cwk, ci, CHUNK))
        v = _silu(_causal_conv_chunk(v_ref, cwv, ci, CHUNK))
        q = _l2norm(q) * (HEAD_DIM ** -0.5)
        k = _l2norm(k)
        sm = sm_ref[rows, :]
        beta = _lane_select(_sigmoid(sm), h)
        g = _lane_select(neg_a * _softplus(sm + dtb), n_heads + h)
        gb = jnp.broadcast_to(g, (CHUNK, LANES))
        big_g = jnp.dot(tri, gb, preferred_element_type=F32, precision=HIGHEST)
        g_row = big_g.T[:CHUNK, :]
        g_col = big_g[:, :CHUNK]
        gamma = jnp.where(incl, jnp.exp(jnp.where(incl, g_col - g_row, 0.0)), 0.0)
        kb = k.astype(BF16)
        kk = _dot_nt(kb, kb)
        mm = jnp.where(strict, beta * kk * gamma, 0.0)
        tinv = _unit_lower_inverse(mm)
        eg = jnp.exp(big_g)
        u = jnp.dot(tinv, v * beta, preferred_element_type=F32, precision=HIGHEST)
        w = jnp.dot(tinv, k * (beta * eg), preferred_element_type=F32, precision=HIGHEST)
        attn = _dot_nt(q.astype(BF16), kb) * gamma
        g_end = big_g[CHUNK - 1:CHUNK, :]
        u_s[rows, :] = u
        w_s[rows, :] = w
        qe_s[rows, :] = q * eg
        kd_s[rows, :] = k * jnp.exp(g_end - big_g)
        attn_s[ci] = attn
        gend_s[ci] = jnp.broadcast_to(jnp.exp(g_end), (SUBLANES, LANES))
        return carry

    lax.fori_loop(0, n_chunks, phase1, 0, unroll=2)

    st_s[...] = jnp.zeros_like(st_s)
    gain = gain_ref[...]

    def phase2(ci, carry):
        rows = pl.ds(pl.multiple_of(ci * CHUNK, CHUNK), CHUNK)
        s = st_s[...]
        sb = s.astype(BF16)
        v_new = u_s[rows, :] - _dot(w_s[rows, :].astype(BF16), sb)
        vb = v_new.astype(BF16)
        o = _dot(qe_s[rows, :].astype(BF16), sb) + _dot(attn_s[ci].astype(BF16), vb)
        st_s[...] = s * gend_s[ci][0:1, :] + _dot_tn(kd_s[rows, :].astype(BF16), vb)
        o_ref[rows, :] = _gated_head_norm(o, gt_ref[rows, :], gain).astype(o_ref.dtype)
        return carry

    lax.fori_loop(0, n_chunks, phase2, 0)
    s_ref[...] = st_s[...]


def _dn_prompt(proj, sm, cw, alog_vec, dtb_vec, gain, *, batch, n_heads, slab_q, slab_g):
    seq = proj.shape[2]
    n_chunks = seq // CHUNK

    def slab_spec(first):
        return pl.BlockSpec((None, None, seq, LANES), lambda b, h: (first + h, b, 0, 0))

    def cw_spec(first):
        return pl.BlockSpec((None, DN_CONV, LANES), lambda b, h: (first + h, 0, 0))

    vec_spec = pl.BlockSpec((1, LANES), lambda b, h: (0, 0))
    return pl.pallas_call(
        functools.partial(_dn_prompt_kernel, n_heads=n_heads),
        out_shape=(jax.ShapeDtypeStruct((batch, seq, n_heads * LANES), BF16),
                   jax.ShapeDtypeStruct((batch, n_heads, HEAD_DIM, HEAD_DIM), F32)),
        grid=(batch, n_heads),
        in_specs=[
            slab_spec(slab_q), slab_spec(slab_q + n_heads), slab_spec(slab_q + 2 * n_heads), slab_spec(slab_g),
            pl.BlockSpec((None, seq, LANES), lambda b, h: (b, 0, 0)),
            cw_spec(0), cw_spec(n_heads), cw_spec(2 * n_heads),
            vec_spec, vec_spec, vec_spec,
        ],
        out_specs=(pl.BlockSpec((None, seq, LANES), lambda b, h: (b, 0, h)),
                   pl.BlockSpec((None, None, HEAD_DIM, HEAD_DIM), lambda b, h: (b, h, 0, 0))),
        scratch_shapes=[pltpu.VMEM((seq, LANES), F32)] * 4 + [
            pltpu.VMEM((n_chunks, CHUNK, CHUNK), F32),
            pltpu.VMEM((n_chunks, SUBLANES, LANES), F32),
            pltpu.VMEM((HEAD_DIM, HEAD_DIM), F32),
        ],
        compiler_params=_cparams(("parallel", "parallel")),
        name="dn_prompt",
    )(proj, proj, proj, proj, sm, cw, cw, cw, alog_vec, dtb_vec, gain)


def _hg_lower_bound(logits, layer):
    mx = jnp.max(logits, axis=0, keepdims=True)
    e = jnp.exp(logits - mx)
    p = e / jnp.sum(e, axis=0, keepdims=True)
    lb = jnp.zeros((1, logits.shape[1]), F32)
    for i in range(1, layer + 1):
        lb = lb + p[i:i + 1]
    return lb


def _hg_gates(zf, lb):
    log_f = _logaddexp(jnp.log(jnp.maximum(lb, LB_FLOOR)), jnp.log1p(-lb) + _log_sigmoid(zf))
    k = (1.0 - lb) * _sigmoid(-zf)
    return log_f, k


def _gla_intra(q, k, big_g):
    n = q.shape[0]
    lane = lax.broadcasted_iota(jnp.int32, (SUB, n), 1)
    rowi = lax.broadcasted_iota(jnp.int32, (SUB, LANES), 0)
    out = []
    for bi in range(n // SUB):
        lo = bi * SUB
        qi = q[lo:lo + SUB]
        gi = big_g[lo:lo + SUB]
        ki = k[lo:lo + SUB]
        blk = jnp.zeros((SUB, n), F32)
        for j in range(SUB):
            keep = rowi >= j
            dec = jnp.exp(jnp.where(keep, gi - gi[j:j + 1], 0.0))
            col = jnp.sum(jnp.where(keep, qi * ki[j:j + 1] * dec, 0.0), axis=1, keepdims=True)
            blk = jnp.where(lane == lo + j, col, blk)
        if bi > 0:
            ref = big_g[lo - 1:lo]
            qt = qi * jnp.exp(gi - ref)
            kt = k[:lo] * jnp.exp(ref - big_g[:lo])
            off = _dot_nt(qt, kt, precision=HIGHEST)
            blk = blk + jnp.concatenate([off, jnp.zeros((SUB, n - lo), F32)], axis=1)
        out.append(blk)
    return jnp.concatenate(out, axis=0)


def _gla_prompt_kernel(*refs, mode, layer):
    if mode == "gla":
        (q_ref, k_ref, v_ref, gt_ref, sm_ref, wg_ref, bias_ref, gain_ref, o_ref, s_ref, st_s) = refs
    else:
        (q_ref, k_ref, v_ref, gt_ref, lbl_ref, gain_ref, o_ref, s_ref, st_s) = refs
        lb = _hg_lower_bound(lbl_ref[...], layer)
    seq = q_ref.shape[0]
    n_chunks = seq // CHUNK
    r, c = _tri_incl(CHUNK)
    tri = (r >= c).astype(F32)
    gain = gain_ref[...]
    st_s[...] = jnp.zeros_like(st_s)

    def body(ci, carry):
        rows = pl.ds(pl.multiple_of(ci * CHUNK, CHUNK), CHUNK)
        if mode == "gla":
            q = q_ref[rows, :] * (HEAD_DIM ** -0.5)
            k = k_ref[rows, :]
            z = _dot(sm_ref[rows, :].astype(BF16), wg_ref[...]) + bias_ref[...]
            g = _log_sigmoid(z) * (1.0 / GLA_NORMALIZER)
        else:
            q = _silu(q_ref[rows, :])
            g, k = _hg_gates(k_ref[rows, :], lb)
        v = v_ref[rows, :]
        big_g = jnp.dot(tri, g, preferred_element_type=F32, precision=HIGHEST)
        a = _gla_intra(q, k, big_g)
        g_end = big_g[CHUNK - 1:CHUNK, :]
        qe = q * jnp.exp(big_g)
        kd = k * jnp.exp(g_end - big_g)
        st = st_s[...]
        vb = v.astype(BF16)
        o = _dot_nt(qe.astype(BF16), st.astype(BF16)) + _dot(a.astype(BF16), vb)
        st_s[...] = st * jnp.exp(g_end) + _dot_tn(vb, kd.astype(BF16))
        o_ref[rows, :] = _gated_head_norm(o, gt_ref[rows, :], gain).astype(o_ref.dtype)
        return carry

    lax.fori_loop(0, n_chunks, body, 0)
    s_ref[...] = st_s[...].T


def _gla_prompt(proj, *, mode, layer, batch, n_heads, slabs, sm=None, wg=None, bias=None, lbl=None, gain=None):
    seq = proj.shape[2]

    def slab_spec(first):
        return pl.BlockSpec((None, None, seq, LANES), lambda b, h: (first + h, b, 0, 0))

    in_specs = [slab_spec(s) for s in slabs]
    args = [proj] * 4
    if mode == "gla":
        in_specs += [
            pl.BlockSpec((None, seq, LANES), lambda b, h: (b, 0, 0)),
            pl.BlockSpec((None, LANES, LANES), lambda b, h: (h, 0, 0)),
            pl.BlockSpec((None, 1, LANES), lambda b, h: (h, 0, 0)),
        ]
        args += [sm, wg, bias]
    else:
        in_specs += [pl.BlockSpec((None, lbl.shape[1], LANES), lambda b, h: (h, 0, 0))]
        args += [lbl]
    in_specs.append(pl.BlockSpec((1, LANES), lambda b, h: (0, 0)))
    args.append(gain)
    return pl.pallas_call(
        functools.partial(_gla_prompt_kernel, mode=mode, layer=layer),
        out_shape=(jax.ShapeDtypeStruct((batch, seq, n_heads * LANES), BF16),
                   jax.ShapeDtypeStruct((batch, n_heads, HEAD_DIM, HEAD_DIM), F32)),
        grid=(batch, n_heads),
        in_specs=in_specs,
        out_specs=(pl.BlockSpec((None, seq, LANES), lambda b, h: (b, 0, h)),
                   pl.BlockSpec((None, None, HEAD_DIM, HEAD_DIM), lambda b, h: (b, h, 0, 0))),
        scratch_shapes=[pltpu.VMEM((HEAD_DIM, HEAD_DIM), F32)],
        compiler_params=_cparams(("parallel", "parallel")),
        name=mode + "_prompt",
    )(*args)


def _diag_extract(row, n_rows, offset):
    r = lax.broadcasted_iota(jnp.int32, (n_rows, LANES), 0)
    l = lax.broadcasted_iota(jnp.int32, (n_rows, LANES), 1)
    return jnp.sum(jnp.where(l == r + offset, jnp.broadcast_to(row, (n_rows, LANES)), 0.0), axis=1, keepdims=True)


def _columns(x, n_pad):
    n = x.shape[0]
    if n_pad > n:
        x = jnp.concatenate([x, jnp.zeros((n_pad - n, x.shape[1]), F32)], axis=0)
    return x.T


def _dn_decode_kernel(p_ref, sm_ref, conv_ref, s_ref, cw_ref, alog_ref, dtb_ref, gain_ref,
                      o_ref, conv_out_ref, s_out_ref, *, n_heads, slab_g, bb):
    nh = n_heads
    pad = -(-nh // SUBLANES) * SUBLANES
    cw = cw_ref[...]
    neg_a = -jnp.exp(alog_ref[...])
    dtb = dtb_ref[...]
    gain = gain_ref[...]
    for b in range(bb):
        x = p_ref[b, 0:3 * nh, :]
        y = cw[DN_CONV - 1] * x
        for i in range(DN_CONV - 1):
            y = y + cw[i] * conv_ref[b, i]
        for i in range(DN_CONV - 2):
            conv_out_ref[b, i] = conv_ref[b, i + 1]
        conv_out_ref[b, DN_CONV - 2] = x
        act = _silu(y)
        q = _l2norm(act[0:nh]) * (HEAD_DIM ** -0.5)
        k = _l2norm(act[nh:2 * nh])
        v = act[2 * nh:3 * nh]
        srow = sm_ref[b:b + 1, :]
        beta = _diag_extract(_sigmoid(srow), nh, 0)
        g = _diag_extract(neg_a * _softplus(srow + dtb), nh, nh)
        eg = jnp.exp(g)
        k_t = _columns(k, pad)
        q_t = _columns(q, pad)
        w_t = _columns(k * (beta * eg), pad)
        u = v * beta
        outs = []
        for h in range(nh):
            s = s_ref[b, h]
            v_new = u[h:h + 1] - jnp.sum(w_t[:, h:h + 1] * s, axis=0, keepdims=True)
            s_new = s * eg[h:h + 1] + k_t[:, h:h + 1] * v_new
            s_out_ref[b, h] = s_new
            outs.append(jnp.sum(q_t[:, h:h + 1] * s_new, axis=0, keepdims=True))
        o = jnp.concatenate(outs, axis=0)
        gate = p_ref[b, slab_g:slab_g + nh, :]
        o_ref[b] = _gated_head_norm(o, gate, gain).astype(o_ref.dtype)


def _dn_decode(proj, sm, conv_state, s_state, cw, alog_vec, dtb_vec, gain, *, n_heads, slab_g, bb):
    batch, n_slabs, _ = proj.shape
    vec_spec = pl.BlockSpec((1, LANES), lambda i: (0, 0))
    return pl.pallas_call(
        functools.partial(_dn_decode_kernel, n_heads=n_heads, slab_g=slab_g, bb=bb),
        out_shape=(jax.ShapeDtypeStruct((batch, n_heads, LANES), BF16),
                   jax.ShapeDtypeStruct(conv_state.shape, F32),
                   jax.ShapeDtypeStruct(s_state.shape, F32)),
        grid=(batch // bb,),
        in_specs=[
            pl.BlockSpec((bb, n_slabs, LANES), lambda i: (i, 0, 0)),
            pl.BlockSpec((bb, LANES), lambda i: (i, 0)),
            pl.BlockSpec((bb,) + conv_state.shape[1:], lambda i: (i, 0, 0, 0)),
            pl.BlockSpec((bb,) + s_state.shape[1:], lambda i: (i, 0, 0, 0)),
            pl.BlockSpec(cw.shape, lambda i: (0, 0, 0)),
            vec_spec, vec_spec, vec_spec,
        ],
        out_specs=(pl.BlockSpec((bb, n_heads, LANES), lambda i: (i, 0, 0)),
                   pl.BlockSpec((bb,) + conv_state.shape[1:], lambda i: (i, 0, 0, 0)),
                   pl.BlockSpec((bb,) + s_state.shape[1:], lambda i: (i, 0, 0, 0))),
        compiler_params=_cparams(("parallel",)),
        name="dn_decode",
    )(proj, sm, conv_state, s_state, cw, alog_vec, dtb_vec, gain)


def _gla_gate_decode_kernel(sm_ref, wg_ref, bias_ref, o_ref):
    z = _dot(sm_ref[...].astype(BF16), wg_ref[...]) + bias_ref[...]
    o_ref[...] = _log_sigmoid(z) * (1.0 / GLA_NORMALIZER)


def _gla_gate_decode(sm, wg_full, bias):
    m = sm.shape[0]
    n = wg_full.shape[1]
    return pl.pallas_call(
        _gla_gate_decode_kernel,
        out_shape=jax.ShapeDtypeStruct((m, n), F32),
        name="gla_gate_decode",
    )(sm, wg_full, bias)


def _gla_decode_kernel(*refs, mode, layer, n_heads, slabs, bb):
    if mode == "gla":
        p_ref, ga_ref, s_ref, gain_ref, o_ref, s_out_ref = refs
    else:
        p_ref, lbl_ref, s_ref, gain_ref, o_ref, s_out_ref = refs
    nh = n_heads
    pad = -(-nh // SUBLANES) * SUBLANES
    sq, sk, sv, sg = slabs
    gain = gain_ref[...]
    for b in range(bb):
        if mode == "gla":
            q = p_ref[b, sq:sq + nh, :] * (HEAD_DIM ** -0.5)
            k = p_ref[b, sk:sk + nh, :]
            g = ga_ref[b]
        else:
            q = _silu(p_ref[b, sq:sq + nh, :])
            lb = jnp.concatenate([_hg_lower_bound(lbl_ref[h], layer) for h in range(nh)], axis=0)
            g, k = _hg_gates(p_ref[b, sk:sk + nh, :], lb)
        v = p_ref[b, sv:sv + nh, :]
        a_t = _columns(jnp.exp(g), pad)
        k_t = _columns(k, pad)
        q_t = _columns(q, pad)
        outs = []
        for h in range(nh):
            s_new = a_t[:, h:h + 1] * s_ref[b, h] + k_t[:, h:h + 1] * v[h:h + 1]
            s_out_ref[b, h] = s_new
            outs.append(jnp.sum(q_t[:, h:h + 1] * s_new, axis=0, keepdims=True))
        o = jnp.concatenate(outs, axis=0)
        o_ref[b] = _gated_head_norm(o, p_ref[b, sg:sg + nh, :], gain).astype(o_ref.dtype)


def _gla_decode(proj, s_state, gain, *, mode, layer, n_heads, slabs, bb, ga=None, lbl=None):
    batch, n_slabs, _ = proj.shape
    if mode == "gla":
        extra, extra_spec = ga, pl.BlockSpec((bb, n_heads, LANES), lambda i: (i, 0, 0))
    else:
        extra, extra_spec = lbl, pl.BlockSpec(lbl.shape, lambda i: (0, 0, 0))
    return pl.pallas_call(
        functools.partial(_gla_decode_kernel, mode=mode, layer=layer, n_heads=n_heads, slabs=slabs, bb=bb),
        out_shape=(jax.ShapeDtypeStruct((batch, n_heads, LANES), BF16), jax.ShapeDtypeStruct(s_state.shape, F32)),
        grid=(batch // bb,),
        in_specs=[
            pl.BlockSpec((bb, n_slabs, LANES), lambda i: (i, 0, 0)),
            extra_spec,
            pl.BlockSpec((bb,) + s_state.shape[1:], lambda i: (i, 0, 0, 0)),
            pl.BlockSpec((1, LANES), lambda i: (0, 0)),
        ],
        out_specs=(pl.BlockSpec((bb, n_heads, LANES), lambda i: (i, 0, 0)),
                   pl.BlockSpec((bb,) + s_state.shape[1:], lambda i: (i, 0, 0, 0))),
        compiler_params=_cparams(("parallel",)),
        name=mode + "_decode",
    )(proj, extra, s_state, gain)


def _layer_weights(l, w_in, dn_conv_w, dn_a_log, dn_dt_bias, gla_w_gate, gla_gate_bias, w_out,
                   ffn_w_gate, ffn_w_up, ffn_conv_w, ffn_conv_b, ffn_w_down, dims):
    dn_h, gla_h, hg_h = dims["dn_h"], dims["gla_h"], dims["hg_h"]
    dn_dim, gla_dim, hg_dim = dn_h * HEAD_DIM, gla_h * HEAD_DIM, hg_h * HEAD_DIM
    sizes = (3 * dn_dim, dn_h, dn_h, dn_dim, gla_dim, gla_dim, gla_dim, GLA_LOWRANK, gla_dim,
             hg_dim, hg_dim, hg_dim, hg_dim)
    offs = [0]
    for s in sizes:
        offs.append(offs[-1] + s)
    wl = w_in[l]
    piece = lambda i: wl[:, offs[i]:offs[i + 1]]
    wide = [0, 3, 4, 5, 6, 8, 9, 10, 11, 12]
    w_big = jnp.concatenate([piece(i) for i in wide], axis=1).astype(BF16)
    small = jnp.concatenate([piece(1), piece(2), piece(7)], axis=1)
    n_small = small.shape[1]
    w_small = jnp.pad(small, ((0, 0), (0, LANES - n_small))).astype(BF16)

    d_ff = ffn_w_gate.shape[2]
    d_ffp = -(-d_ff // FF_ALIGN) * FF_ALIGN
    padc = ((0, 0), (0, d_ffp - d_ff))
    wo = w_out[l].astype(BF16)
    lr0 = 2 * dn_h
    wg_rows = jnp.pad(gla_w_gate[l], ((lr0, LANES - lr0 - GLA_LOWRANK), (0, 0))).astype(BF16)
    return dict(
        w_big=w_big, w_small=w_small,
        wo=(wo[:dn_dim], wo[dn_dim:dn_dim + gla_dim], wo[dn_dim + gla_dim:]),
        dn_cw=dn_conv_w[l].reshape(DN_CONV, 3 * dn_h, HEAD_DIM),
        alog_vec=jnp.pad(dn_a_log[l], (dn_h, LANES - 2 * dn_h)).reshape(1, LANES),
        dtb_vec=jnp.pad(dn_dt_bias[l], (dn_h, LANES - 2 * dn_h)).reshape(1, LANES),
        gla_wg_full=wg_rows,
        gla_wg_heads=wg_rows.reshape(LANES, gla_h, HEAD_DIM).transpose(1, 0, 2),
        gla_bias=gla_gate_bias[l].reshape(1, gla_dim),
        ffn_wg=jnp.pad(ffn_w_gate[l], padc).astype(BF16),
        ffn_wu=jnp.pad(ffn_w_up[l], padc).astype(BF16),
        ffn_wd=jnp.pad(ffn_w_down[l], ((0, d_ffp - d_ff), (0, 0))).astype(BF16),
        ffn_cw=jnp.pad(ffn_conv_w[l], padc),
        ffn_cb=jnp.pad(ffn_conv_b[l], (0, d_ffp - d_ff)).reshape(1, d_ffp),
        d_ff=d_ff, d_ffp=d_ffp,
    )


def _slab_table(dims):
    dn_h, gla_h, hg_h = dims["dn_h"], dims["gla_h"], dims["hg_h"]
    names = [("dn_qkv", 3 * dn_h), ("dn_g", dn_h), ("gla_q", gla_h), ("gla_k", gla_h), ("gla_v", gla_h),
             ("gla_g", gla_h), ("hg_q", hg_h), ("hg_f", hg_h), ("hg_i", hg_h), ("hg_g", hg_h)]
    table, pos = {}, 0
    for n, cnt in names:
        table[n] = pos
        pos += cnt
    return table


def _prompt_layer(x, l, lw, norm_mix, dn_norm, gla_norm, hg_norm, hg_lbl, norm_ffn, dims, batch):
    m, d = x.shape
    seq = m // batch
    dn_h, gla_h, hg_h = dims["dn_h"], dims["gla_h"], dims["hg_h"]
    sl = _slab_table(dims)
    h = _rmsnorm(x, norm_mix[l], BF16)
    proj = _matmul([h], [lw["w_big"]], slab=True, tm=1024, tn=1024, name="in_proj")
    sm = _matmul([h], [lw["w_small"]], tm=1024, tn=LANES, name="in_proj_small")
    proj4 = proj.reshape(proj.shape[0], batch, seq, LANES)
    sm3 = sm.reshape(batch, seq, LANES)
    cw = lw["dn_cw"].transpose(1, 0, 2)
    o_dn, s_dn = _dn_prompt(proj4, sm3, cw, lw["alog_vec"], lw["dtb_vec"], dn_norm[l].reshape(1, LANES),
                            batch=batch, n_heads=dn_h, slab_q=sl["dn_qkv"], slab_g=sl["dn_g"])
    o_gla, s_gla = _gla_prompt(proj4, mode="gla", layer=l, batch=batch, n_heads=gla_h,
                               slabs=(sl["gla_q"], sl["gla_k"], sl["gla_v"], sl["gla_g"]),
                               sm=sm3, wg=lw["gla_wg_heads"], bias=lw["gla_bias"].reshape(gla_h, 1, LANES),
                               gain=gla_norm[l].reshape(1, LANES))
    o_hg, s_hg = _gla_prompt(proj4, mode="hg", layer=l, batch=batch, n_heads=hg_h,
                             slabs=(sl["hg_q"], sl["hg_f"], sl["hg_i"], sl["hg_g"]),
                             lbl=hg_lbl, gain=hg_norm[l].reshape(1, LANES))
    o_list = [o_dn.reshape(m, -1), o_gla.reshape(m, -1), o_hg.reshape(m, -1)]
    x = _matmul(o_list, list(lw["wo"]), res=x, tm=1024, tn=512, name="out_proj")
    h2 = _rmsnorm(x, norm_ffn[l], BF16)
    hidden, tail = _ffn_up_prompt(h2, lw["ffn_wg"], lw["ffn_wu"], lw["ffn_cw"], lw["ffn_cb"],
                                  batch=batch, tm=512, tn=512)
    x = _matmul_ktiled(hidden, lw["ffn_wd"], x, tm=1024, tn=1024, tk=lw["d_ffp"] // 4, name="ffn_down")
    n_qkv = 3 * dn_h
    conv_dn = proj4[sl["dn_qkv"]:sl["dn_qkv"] + n_qkv, :, seq - (DN_CONV - 1):, :]
    conv_dn = conv_dn.transpose(1, 2, 0, 3).reshape(batch, DN_CONV - 1, n_qkv * LANES)
    conv_ffn = tail[:, SUBLANES - (FFN_CONV - 1):, :lw["d_ff"]]
    return x, (conv_dn, s_dn, s_gla, s_hg, conv_ffn)


def _decode_layer(x, l, lw, st_dn_conv, st_dn, st_gla, st_hg, st_ffn, norm_mix, dn_norm, gla_norm, hg_norm,
                  hg_lbl, norm_ffn, dims):
    m, d = x.shape
    dn_h, gla_h, hg_h = dims["dn_h"], dims["gla_h"], dims["hg_h"]
    sl = _slab_table(dims)
    bb = SUBLANES
    h = _rmsnorm(x, norm_mix[l], BF16)
    proj = _matmul([h], [lw["w_big"]], tm=m, tn=1024, name="in_proj_dec")
    sm = _matmul([h], [lw["w_small"]], tm=m, tn=LANES, name="in_proj_small_dec")
    proj3 = proj.reshape(m, -1, LANES)
    conv_in = st_dn_conv[l].reshape(m, DN_CONV - 1, 3 * dn_h, LANES)
    o_dn, conv_dn, s_dn = _dn_decode(proj3, sm, conv_in, st_dn[l], lw["dn_cw"], lw["alog_vec"], lw["dtb_vec"],
                                     dn_norm[l].reshape(1, LANES), n_heads=dn_h, slab_g=sl["dn_g"], bb=bb)
    ga = _gla_gate_decode(sm, lw["gla_wg_full"], lw["gla_bias"]).reshape(m, gla_h, LANES)
    o_gla, s_gla = _gla_decode(proj3, st_gla[l], gla_norm[l].reshape(1, LANES), mode="gla", layer=l, n_heads=gla_h,
                               slabs=(sl["gla_q"], sl["gla_k"], sl["gla_v"], sl["gla_g"]), bb=bb, ga=ga)
    o_hg, s_hg = _gla_decode(proj3, st_hg[l], hg_norm[l].reshape(1, LANES), mode="hg", layer=l, n_heads=hg_h,
                             slabs=(sl["hg_q"], sl["hg_f"], sl["hg_i"], sl["hg_g"]), bb=bb, lbl=hg_lbl)
    o_list = [o_dn.reshape(m, -1), o_gla.reshape(m, -1), o_hg.reshape(m, -1)]
    x = _matmul(o_list, list(lw["wo"]), res=x, tm=m, tn=1024, name="out_proj_dec")
    h2 = _rmsnorm(x, norm_ffn[l], BF16)
    d_ff, d_ffp = lw["d_ff"], lw["d_ffp"]
    st2d = jnp.pad(st_ffn[l], ((0, 0), (0, 0), (0, d_ffp - d_ff))).reshape(m, (FFN_CONV - 1) * d_ffp)
    hidden, gate = _ffn_up_decode(h2, lw["ffn_wg"], lw["ffn_wu"], st2d, lw["ffn_cw"], lw["ffn_cb"], tn=512)
    x = _matmul_ktiled(hidden, lw["ffn_wd"], x, tm=m, tn=1024, tk=d_ffp // 4, name="ffn_down_dec")
    conv_ffn = jnp.stack([st_ffn[l][:, 1, :], gate[:, :d_ff]], axis=1)
    return x, (conv_dn.reshape(m, DN_CONV - 1, -1), s_dn, s_gla, s_hg, conv_ffn)


def kernel(x_prompt, x_sample, state_dn_conv, state_dn, state_gla, state_hg, state_ffn_conv, norm_mix, w_in, dn_conv_w, dn_a_log, dn_dt_bias, dn_norm, gla_w_gate, gla_gate_bias, gla_norm, hg_lb_logits, hg_norm, w_out, norm_ffn, ffn_w_gate, ffn_w_up, ffn_conv_w, ffn_conv_b, ffn_w_down, norm_final):
    batch, seq, d_model = x_prompt.shape
    dec_batch, dec_seq, _ = x_sample.shape
    assert dec_seq == 1 and FFN_CONV == 3
    depth = w_in.shape[0]
    dims = dict(dn_h=state_dn.shape[2], gla_h=state_gla.shape[2], hg_h=state_hg.shape[2])
    hg_lbl = hg_lb_logits.reshape(depth, dims["hg_h"], HEAD_DIM).transpose(1, 0, 2)

    xp = x_prompt.reshape(batch * seq, d_model)
    xs = x_sample.reshape(dec_batch, d_model)
    p_states, s_states = [], []
    for l in range(depth):
        lw = _layer_weights(l, w_in, dn_conv_w, dn_a_log, dn_dt_bias, gla_w_gate, gla_gate_bias, w_out,
                            ffn_w_gate, ffn_w_up, ffn_conv_w, ffn_conv_b, ffn_w_down, dims)
        xp, ps = _prompt_layer(xp, l, lw, norm_mix, dn_norm, gla_norm, hg_norm, hg_lbl, norm_ffn, dims, batch)
        xs, ss = _decode_layer(xs, l, lw, state_dn_conv, state_dn, state_gla, state_hg, state_ffn_conv,
                               norm_mix, dn_norm, gla_norm, hg_norm, hg_lbl, norm_ffn, dims)
        p_states.append(ps)
        s_states.append(ss)
    y_prompt = _rmsnorm(xp, norm_final, F32).reshape(batch, seq, d_model)
    y_sample = _rmsnorm(xs, norm_final, F32).reshape(dec_batch, dec_seq, d_model)
    stack = lambda states, i: jnp.stack([s[i] for s in states], axis=0)
    return ((y_prompt, y_sample) + tuple(stack(p_states, i) for i in range(5))
            + tuple(stack(s_states, i) for i in range(5)))
```

```python
import functools
import math

import jax
import jax.numpy as jnp
from jax import lax
from jax.experimental import pallas as pl
from jax.experimental.pallas import tpu as pltpu

F32 = jnp.float32
BF16 = jnp.bfloat16
HIGHEST = lax.Precision.HIGHEST

HEAD_DIM = 128
DN_CONV = 4
GLA_LOWRANK = 16
GLA_NORMALIZER = 16.0
FFN_CONV = 3
EPS = 1e-6
LB_FLOOR = 1e-30

LANES = 128
SUBLANES = 8
MXU_DIM = 256
VMEM_LIMIT_BYTES = 56 * 1024 * 1024

CHUNK = 64
SUB = 16
FF_ALIGN = 1024


def _cparams(sem):
    return pltpu.CompilerParams(dimension_semantics=sem, vmem_limit_bytes=VMEM_LIMIT_BYTES)


def _sigmoid(x):
    return 1.0 / (1.0 + jnp.exp(-x))


def _silu(x):
    return x * _sigmoid(x)


def _softplus(x):
    return jnp.maximum(x, 0.0) + jnp.log1p(jnp.exp(-jnp.abs(x)))


def _log_sigmoid(x):
    return -_softplus(-x)


def _logaddexp(a, b):
    return jnp.maximum(a, b) + jnp.log1p(jnp.exp(-jnp.abs(a - b)))


def _dot(a, b):
    return jnp.dot(a, b, preferred_element_type=F32)


def _dot_nt(a, b, precision=None):
    return lax.dot_general(a, b, (((1,), (1,)), ((), ())), preferred_element_type=F32, precision=precision)


def _dot_tn(a, b):
    return lax.dot_general(a, b, (((0,), (0,)), ((), ())), preferred_element_type=F32)


def _tri_incl(n):
    r = lax.broadcasted_iota(jnp.int32, (n, n), 0)
    c = lax.broadcasted_iota(jnp.int32, (n, n), 1)
    return r, c


def _gated_head_norm(o, gate, gain):
    o = o * lax.rsqrt(jnp.mean(o * o, axis=-1, keepdims=True) + EPS) * gain
    return o * _silu(gate)


def _rmsnorm_kernel(x_ref, w_ref, o_ref):
    x = x_ref[...]
    y = x * lax.rsqrt(jnp.mean(x * x, axis=-1, keepdims=True) + EPS)
    o_ref[...] = (y * w_ref[...]).astype(o_ref.dtype)


def _rmsnorm(x, w, out_dtype):
    m, d = x.shape
    tm = min(m, 256)
    return pl.pallas_call(
        _rmsnorm_kernel,
        out_shape=jax.ShapeDtypeStruct((m, d), out_dtype),
        grid=(m // tm,),
        in_specs=[pl.BlockSpec((tm, d), lambda i: (i, 0)), pl.BlockSpec((1, d), lambda i: (0, 0))],
        out_specs=pl.BlockSpec((tm, d), lambda i: (i, 0)),
        compiler_params=_cparams(("parallel",)),
        name="rmsnorm",
    )(x, w.reshape(1, d))


def _mm_fullk_kernel(*refs, n_lhs, has_res, slab):
    lhs = refs[:n_lhs]
    rhs = refs[n_lhs:2 * n_lhs]
    res_ref = refs[2 * n_lhs] if has_res else None
    o_ref = refs[2 * n_lhs + (1 if has_res else 0)]
    acc = _dot(lhs[0][...], rhs[0][...])
    for a, b in zip(lhs[1:], rhs[1:]):
        acc = acc + _dot(a[...], b[...])
    if has_res:
        acc = res_ref[...] + acc
    if slab:
        for j in range(o_ref.shape[0]):
            o_ref[j] = acc[:, j * LANES:(j + 1) * LANES]
    else:
        o_ref[...] = acc.astype(o_ref.dtype)


def _matmul(lhs_list, rhs_list, *, res=None, slab=False, tm, tn, name):
    m = lhs_list[0].shape[0]
    n = rhs_list[0].shape[1]
    tm = min(tm, m)
    tn = min(tn, n)
    in_specs = [pl.BlockSpec((tm, a.shape[1]), lambda i, j: (i, 0)) for a in lhs_list]
    in_specs += [pl.BlockSpec((b.shape[0], tn), lambda i, j: (0, j)) for b in rhs_list]
    args = list(lhs_list) + list(rhs_list)
    if res is not None:
        in_specs.append(pl.BlockSpec((tm, tn), lambda i, j: (i, j)))
        args.append(res)
    if slab:
        out_shape = jax.ShapeDtypeStruct((n // LANES, m, LANES), F32)
        out_spec = pl.BlockSpec((tn // LANES, tm, LANES), lambda i, j: (j, i, 0))
    else:
        out_shape = jax.ShapeDtypeStruct((m, n), F32)
        out_spec = pl.BlockSpec((tm, tn), lambda i, j: (i, j))
    return pl.pallas_call(
        functools.partial(_mm_fullk_kernel, n_lhs=len(lhs_list), has_res=res is not None, slab=slab),
        out_shape=out_shape,
        grid=(m // tm, n // tn),
        in_specs=in_specs,
        out_specs=out_spec,
        compiler_params=_cparams(("parallel", "arbitrary")),
        name=name,
    )(*args)


def _mm_ktiled_kernel(x_ref, w_ref, res_ref, o_ref, acc_ref):
    k = pl.program_id(2)

    @pl.when(k == 0)
    def _():
        acc_ref[...] = res_ref[...]

    acc_ref[...] += _dot(x_ref[...], w_ref[...])

    @pl.when(k == pl.num_programs(2) - 1)
    def _():
        o_ref[...] = acc_ref[...]


def _matmul_ktiled(x, w, res, *, tm, tn, tk, name):
    m, kdim = x.shape
    n = w.shape[1]
    tm = min(tm, m)
    tn = min(tn, n)
    return pl.pallas_call(
        _mm_ktiled_kernel,
        out_shape=jax.ShapeDtypeStruct((m, n), F32),
        grid=(m // tm, n // tn, kdim // tk),
        in_specs=[
            pl.BlockSpec((tm, tk), lambda i, j, k: (i, k)),
            pl.BlockSpec((tk, tn), lambda i, j, k: (k, j)),
            pl.BlockSpec((tm, tn), lambda i, j, k: (i, j)),
        ],
        out_specs=pl.BlockSpec((tm, tn), lambda i, j, k: (i, j)),
        scratch_shapes=[pltpu.VMEM((tm, tn), F32)],
        compiler_params=_cparams(("parallel", "parallel", "arbitrary")),
        name=name,
    )(x, w, res)


def _ffn_up_prompt_kernel(x_ref, wg_ref, wu_ref, cw_ref, cb_ref, hid_ref, tail_ref, carry_ref, *, tiles_per_seq):
    m = pl.program_id(1)
    x = x_ref[...]
    gate = _dot(x, wg_ref[...])
    up = _dot(x, wu_ref[...])
    tm = gate.shape[0]

    @pl.when((m % tiles_per_seq) == 0)
    def _():
        carry_ref[...] = jnp.zeros_like(carry_ref)

    cat = jnp.concatenate([carry_ref[...], gate], axis=0)
    cw = cw_ref[...]
    y = cw[2:3] * gate
    for i in range(FFN_CONV - 1):
        back = FFN_CONV - 1 - i
        y = y + cw[i:i + 1] * cat[SUBLANES - back:SUBLANES - back + tm]
    hid_ref[...] = (_silu(y + cb_ref[...]) * up).astype(hid_ref.dtype)
    last = gate[tm - SUBLANES:]
    carry_ref[...] = last
    tail_ref[...] = last


def _ffn_up_prompt(h, wg, wu, cw, cb, *, batch, tm, tn):
    m, d = h.shape
    n = wg.shape[1]
    seq = m // batch
    tm = min(tm, seq)
    tiles_per_seq = seq // tm
    return pl.pallas_call(
        functools.partial(_ffn_up_prompt_kernel, tiles_per_seq=tiles_per_seq),
        out_shape=(jax.ShapeDtypeStruct((m, n), BF16), jax.ShapeDtypeStruct((batch, SUBLANES, n), F32)),
        grid=(n // tn, m // tm),
        in_specs=[
            pl.BlockSpec((tm, d), lambda j, i: (i, 0)),
            pl.BlockSpec((d, tn), lambda j, i: (0, j)),
            pl.BlockSpec((d, tn), lambda j, i: (0, j)),
            pl.BlockSpec((FFN_CONV, tn), lambda j, i: (0, j)),
            pl.BlockSpec((1, tn), lambda j, i: (0, j)),
        ],
        out_specs=(
            pl.BlockSpec((tm, tn), lambda j, i: (i, j)),
            pl.BlockSpec((None, SUBLANES, tn), lambda j, i: (i // tiles_per_seq, 0, j)),
        ),
        scratch_shapes=[pltpu.VMEM((SUBLANES, tn), F32)],
        compiler_params=_cparams(("parallel", "arbitrary")),
        name="ffn_up_prompt",
    )(h, wg, wu, cw, cb)


def _ffn_up_decode_kernel(x_ref, wg_ref, wu_ref, s0_ref, s1_ref, cw_ref, cb_ref, hid_ref, gate_ref):
    x = x_ref[...]
    gate = _dot(x, wg_ref[...])
    up = _dot(x, wu_ref[...])
    cw = cw_ref[...]
    y = cw[0:1] * s0_ref[...] + cw[1:2] * s1_ref[...] + cw[2:3] * gate
    hid_ref[...] = (_silu(y + cb_ref[...]) * up).astype(hid_ref.dtype)
    gate_ref[...] = gate


def _ffn_up_decode(h, wg, wu, state2d, cw, cb, *, tn):
    m, d = h.shape
    n = wg.shape[1]
    nt = n // tn
    return pl.pallas_call(
        _ffn_up_decode_kernel,
        out_shape=(jax.ShapeDtypeStruct((m, n), BF16), jax.ShapeDtypeStruct((m, n), F32)),
        grid=(nt,),
        in_specs=[
            pl.BlockSpec((m, d), lambda j: (0, 0)),
            pl.BlockSpec((d, tn), lambda j: (0, j)),
            pl.BlockSpec((d, tn), lambda j: (0, j)),
            pl.BlockSpec((m, tn), lambda j: (0, j)),
            pl.BlockSpec((m, tn), lambda j: (0, nt + j)),
            pl.BlockSpec((FFN_CONV, tn), lambda j: (0, j)),
            pl.BlockSpec((1, tn), lambda j: (0, j)),
        ],
        out_specs=(pl.BlockSpec((m, tn), lambda j: (0, j)), pl.BlockSpec((m, tn), lambda j: (0, j))),
        compiler_params=_cparams(("parallel",)),
        name="ffn_up_decode",
    )(h, wg, wu, state2d, state2d, cw, cb)


def _lane_select(x, lane_index):
    lane = lax.broadcasted_iota(jnp.int32, x.shape, 1)
    return jnp.sum(jnp.where(lane == lane_index, x, 0.0), axis=1, keepdims=True)


def _causal_conv_chunk(x_ref, cw, c, rows):
    start = pl.multiple_of(c * rows, rows)
    xc = x_ref[pl.ds(start, rows), :]
    pstart = pl.multiple_of(jnp.maximum(c * rows - SUBLANES, 0), SUBLANES)
    xp = x_ref[pl.ds(pstart, SUBLANES), :]
    xp = jnp.where(c == 0, 0.0, xp)
    cat = jnp.concatenate([xp, xc], axis=0)
    y = cw[DN_CONV - 1:DN_CONV] * xc
    for i in range(DN_CONV - 1):
        back = DN_CONV - 1 - i
        y = y + cw[i:i + 1] * cat[SUBLANES - back:SUBLANES - back + rows]
    return y


def _l2norm(x):
    return x * lax.rsqrt(jnp.sum(x * x, axis=-1, keepdims=True) + EPS)


def _split2(x):
    hi = x.astype(BF16)
    lo = (x - hi.astype(F32)).astype(BF16)
    return hi, lo


def _split3(x):
    hi = x.astype(BF16)
    r1 = x - hi.astype(F32)
    mid = r1.astype(BF16)
    lo = (r1 - mid.astype(F32)).astype(BF16)
    return hi, mid, lo


def _dot3(a, b, dot=_dot):
    a_hi, a_lo = _split2(a)
    b_hi, b_lo = _split2(b)
    return dot(a_hi, b_hi) + (dot(a_hi, b_lo) + dot(a_lo, b_hi))


def _dot2(a, b):
    b_hi, b_lo = _split2(b)
    a_b = a.astype(BF16)
    return _dot(a_b, b_hi) + _dot(a_b, b_lo)


def _cumsum_rows(tri_bf16, g):
    hi, mid, lo = _split3(g)
    return _dot(tri_bf16, hi) + (_dot(tri_bf16, mid) + _dot(tri_bf16, lo))


def _unit_lower_solve(mms, rhss):
    n = mms[0].shape[0]
    xs = [-mm for mm in mms]
    rs = [rhs + _dot2(x, rhs) for x, rhs in zip(xs, rhss)]
    for _ in range(int(math.log2(n)) - 1):
        xs = [_dot2(x, x) for x in xs]
        rs = [r + _dot2(x, r) for x, r in zip(xs, rs)]
    return rs


DN_HEADS_PER_STEP = 2
DN_CHUNKS_PER_ITER = 4


def _dn_prompt_kernel(q_ref, k_ref, v_ref, gt_ref, sm_ref, cwq_ref, cwk_ref, cwv_ref, alog_ref, dtb_ref, gain_ref,
                      o_ref, s_ref, u_s, w_s, qe_s, kd_s, attn_s, gend_s, st_s, *, n_heads):
    hp_n = q_ref.shape[0]
    head0 = pl.program_id(1) * hp_n
    seq = q_ref.shape[1]
    n_chunks = seq // CHUNK
    r, c = _tri_incl(CHUNK)
    tri = (r >= c).astype(BF16)
    incl = r >= c
    strict = r > c
    neg_a = -jnp.exp(alog_ref[...])
    dtb = dtb_ref[...]

    def phase1(it, carry):
        items = [(hp, it * DN_CHUNKS_PER_ITER + u) for hp in range(hp_n) for u in range(DN_CHUNKS_PER_ITER)]
        pre = []
        for hp, ci in items:
            rows = pl.ds(pl.multiple_of(ci * CHUNK, CHUNK), CHUNK)
            q = _silu(_causal_conv_chunk(q_ref.at[hp], cwq_ref[hp], ci, CHUNK))
            k = _silu(_causal_conv_chunk(k_ref.at[hp], cwk_ref[hp], ci, CHUNK))
            v = _silu(_causal_conv_chunk(v_ref.at[hp], cwv_ref[hp], ci, CHUNK))
            q = _l2norm(q) * (HEAD_DIM ** -0.5)
            k = _l2norm(k)
            sm = sm_ref[rows, :]
            beta = _lane_select(_sigmoid(sm), head0 + hp)
            g = _lane_select(neg_a * _softplus(sm + dtb), n_heads + head0 + hp)
            pre.append((q, k, v, beta, jnp.broadcast_to(g, (CHUNK, LANES))))
        big_gs = [_cumsum_rows(tri, p[4]) for p in pre]
        mms, rhss, gammas, egs = [], [], [], []
        for (q, k, v, beta, _), big_g in zip(pre, big_gs):
            g_row = big_g.T[:CHUNK, :]
            g_col = big_g[:, :CHUNK]
            gamma = jnp.where(incl, jnp.exp(jnp.where(incl, g_col - g_row, 0.0)), 0.0)
            kb = k.astype(BF16)
            mms.append(jnp.where(strict, beta * _dot_nt(kb, kb) * gamma, 0.0))
            eg = jnp.exp(big_g)
            rhss.append(jnp.concatenate([v * beta, k * (beta * eg)], axis=1))
            gammas.append(gamma)
            egs.append(eg)
        uws = _unit_lower_solve(mms, rhss)
        for (hp, ci), (q, k, v, beta, _), big_g, gamma, eg, uw in zip(items, pre, big_gs, gammas, egs, uws):
            rows = pl.ds(pl.multiple_of(ci * CHUNK, CHUNK), CHUNK)
            g_end = big_g[CHUNK - 1:CHUNK, :]
            u_s[hp, rows, :] = uw[:, :HEAD_DIM]
            w_s[hp, rows, :] = uw[:, HEAD_DIM:]
            qe_s[hp, rows, :] = q * eg
            kd_s[hp, rows, :] = k * jnp.exp(g_end - big_g)
            attn_s[hp, ci] = _dot_nt(q.astype(BF16), k.astype(BF16)) * gamma
            gend_s[hp, ci] = jnp.broadcast_to(jnp.exp(g_end), (SUBLANES, LANES))
        return carry

    lax.fori_loop(0, n_chunks // DN_CHUNKS_PER_ITER, phase1, 0)

    st_s[...] = jnp.zeros_like(st_s)
    gain = gain_ref[...]

    def phase2(ci, carry):
        rows = pl.ds(pl.multiple_of(ci * CHUNK, CHUNK), CHUNK)
        ss = [st_s[hp] for hp in range(hp_n)]
        sbs = [s.astype(BF16) for s in ss]
        vbs = [(u_s[hp, rows, :] - _dot(w_s[hp, rows, :].astype(BF16), sbs[hp])).astype(BF16) for hp in range(hp_n)]
        for hp in range(hp_n):
            o = _dot(qe_s[hp, rows, :].astype(BF16), sbs[hp]) + _dot(attn_s[hp, ci].astype(BF16), vbs[hp])
            st_s[hp] = ss[hp] * gend_s[hp, ci][0:1, :] + _dot_tn(kd_s[hp, rows, :].astype(BF16), vbs[hp])
            o_ref[rows, hp * LANES:(hp + 1) * LANES] = _gated_head_norm(o, gt_ref[hp, rows, :], gain).astype(o_ref.dtype)
        return carry

    lax.fori_loop(0, n_chunks, phase2, 0)
    s_ref[...] = st_s[...]


def _dn_prompt(proj, sm, cw, alog_vec, dtb_vec, gain, *, batch, n_heads, slab_q, slab_g):
    seq = proj.shape[2]
    n_chunks = seq // CHUNK
    hp = DN_HEADS_PER_STEP
    assert n_heads % hp == 0 and slab_q % hp == 0 and slab_g % hp == 0 and n_chunks % DN_CHUNKS_PER_ITER == 0

    def slab_spec(first):
        return pl.BlockSpec((hp, None, seq, LANES), lambda b, j: (first // hp + j, b, 0, 0))

    def cw_spec(first):
        return pl.BlockSpec((hp, DN_CONV, LANES), lambda b, j: (first // hp + j, 0, 0))

    vec_spec = pl.BlockSpec((1, LANES), lambda b, j: (0, 0))
    return pl.pallas_call(
        functools.partial(_dn_prompt_kernel, n_heads=n_heads),
        out_shape=(jax.ShapeDtypeStruct((batch, seq, n_heads * LANES), BF16),
                   jax.ShapeDtypeStruct((batch, n_heads, HEAD_DIM, HEAD_DIM), F32)),
        grid=(batch, n_heads // hp),
        in_specs=[
            slab_spec(slab_q), slab_spec(slab_q + n_heads), slab_spec(slab_q + 2 * n_heads), slab_spec(slab_g),
            pl.BlockSpec((None, seq, LANES), lambda b, j: (b, 0, 0)),
            cw_spec(0), cw_spec(n_heads), cw_spec(2 * n_heads),
            vec_spec, vec_spec, vec_spec,
        ],
        out_specs=(pl.BlockSpec((None, seq, hp * LANES), lambda b, j: (b, 0, j)),
                   pl.BlockSpec((None, hp, HEAD_DIM, HEAD_DIM), lambda b, j: (b, j, 0, 0))),
        scratch_shapes=[pltpu.VMEM((hp, seq, LANES), F32)] * 4 + [
            pltpu.VMEM((hp, n_chunks, CHUNK, CHUNK), F32),
            pltpu.VMEM((hp, n_chunks, SUBLANES, LANES), F32),
            pltpu.VMEM((hp, HEAD_DIM, HEAD_DIM), F32),
        ],
        compiler_params=_cparams(("parallel", "parallel")),
        name="dn_prompt",
    )(proj, proj, proj, proj, sm, cw, cw, cw, alog_vec, dtb_vec, gain)


def _hg_lower_bound(logits, layer):
    mx = jnp.max(logits, axis=0, keepdims=True)
    e = jnp.exp(logits - mx)
    p = e / jnp.sum(e, axis=0, keepdims=True)
    lb = jnp.zeros((1, logits.shape[1]), F32)
    for i in range(1, layer + 1):
        lb = lb + p[i:i + 1]
    return lb


def _hg_gates(zf, lb):
    log_f = _logaddexp(jnp.log(jnp.maximum(lb, LB_FLOOR)), jnp.log1p(-lb) + _log_sigmoid(zf))
    k = (1.0 - lb) * _sigmoid(-zf)
    return log_f, k


def _gla_intra(q, k, big_g):
    n = q.shape[0]
    lane = lax.broadcasted_iota(jnp.int32, (SUB, n), 1)
    rowi = lax.broadcasted_iota(jnp.int32, (SUB, LANES), 0)
    out = []
    for bi in range(n // SUB):
        lo = bi * SUB
        qi = q[lo:lo + SUB]
        gi = big_g[lo:lo + SUB]
        ki = k[lo:lo + SUB]
        blk = jnp.zeros((SUB, n), F32)
        for j in range(SUB):
            keep = rowi >= j
            dec = jnp.exp(jnp.where(keep, gi - gi[j:j + 1], 0.0))
            col = jnp.sum(jnp.where(keep, qi * ki[j:j + 1] * dec, 0.0), axis=1, keepdims=True)
            blk = jnp.where(lane == lo + j, col, blk)
        if bi > 0:
            ref = big_g[lo - 1:lo]
            qt = qi * jnp.exp(gi - ref)
            kt = k[:lo] * jnp.exp(ref - big_g[:lo])
            off = _dot3(qt, kt, _dot_nt)
            blk = blk + jnp.concatenate([off, jnp.zeros((SUB, n - lo), F32)], axis=1)
        out.append(blk)
    return jnp.concatenate(out, axis=0)


GLA_CHUNKS_PER_ITER = 4


def _gla_prompt_kernel(*refs, mode, layer):
    if mode == "gla":
        (q_ref, k_ref, v_ref, gt_ref, sm_ref, wg_ref, bias_ref, gain_ref, o_ref, s_ref, st_s) = refs
    else:
        (q_ref, k_ref, v_ref, gt_ref, lbl_ref, gain_ref, o_ref, s_ref, st_s) = refs
        lb = _hg_lower_bound(lbl_ref[...], layer)
    seq = q_ref.shape[0]
    n_chunks = seq // CHUNK
    r, c = _tri_incl(CHUNK)
    tri = (r >= c).astype(BF16)
    gain = gain_ref[...]
    st_s[...] = jnp.zeros_like(st_s)

    def body(it, carry):
        all_rows, pre = [], []
        for u in range(GLA_CHUNKS_PER_ITER):
            rows = pl.ds(pl.multiple_of((it * GLA_CHUNKS_PER_ITER + u) * CHUNK, CHUNK), CHUNK)
            if mode == "gla":
                q = q_ref[rows, :] * (HEAD_DIM ** -0.5)
                k = k_ref[rows, :]
                z = _dot(sm_ref[rows, :].astype(BF16), wg_ref[...]) + bias_ref[...]
                g = _log_sigmoid(z) * (1.0 / GLA_NORMALIZER)
            else:
                q = _silu(q_ref[rows, :])
                g, k = _hg_gates(k_ref[rows, :], lb)
            all_rows.append(rows)
            pre.append((q, k, g))
        big_gs = [_cumsum_rows(tri, g) for _, _, g in pre]
        mids = []
        for (q, k, _), big_g in zip(pre, big_gs):
            g_end = big_g[CHUNK - 1:CHUNK, :]
            mids.append((_gla_intra(q, k, big_g).astype(BF16), (q * jnp.exp(big_g)).astype(BF16),
                         (k * jnp.exp(g_end - big_g)).astype(BF16), jnp.exp(g_end)))
        st = st_s[...]
        for rows, (a, qe, kd, decay_end) in zip(all_rows, mids):
            vb = v_ref[rows, :].astype(BF16)
            o = _dot_nt(qe, st.astype(BF16)) + _dot(a, vb)
            st = st * decay_end + _dot_tn(vb, kd)
            o_ref[rows, :] = _gated_head_norm(o, gt_ref[rows, :], gain).astype(o_ref.dtype)
        st_s[...] = st
        return carry

    lax.fori_loop(0, n_chunks // GLA_CHUNKS_PER_ITER, body, 0)
    s_ref[...] = st_s[...].T


def _gla_prompt(proj, *, mode, layer, batch, n_heads, slabs, sm=None, wg=None, bias=None, lbl=None, gain=None):
    seq = proj.shape[2]

    def slab_spec(first):
        return pl.BlockSpec((None, None, seq, LANES), lambda b, h: (first + h, b, 0, 0))

    in_specs = [slab_spec(s) for s in slabs]
    args = [proj] * 4
    if mode == "gla":
        in_specs += [
            pl.BlockSpec((None, seq, LANES), lambda b, h: (b, 0, 0)),
            pl.BlockSpec((None, LANES, LANES), lambda b, h: (h, 0, 0)),
            pl.BlockSpec((None, 1, LANES), lambda b, h: (h, 0, 0)),
        ]
        args += [sm, wg, bias]
    else:
        in_specs += [pl.BlockSpec((None, lbl.shape[1], LANES), lambda b, h: (h, 0, 0))]
        args += [lbl]
    in_specs.append(pl.BlockSpec((1, LANES), lambda b, h: (0, 0)))
    args.append(gain)
    return pl.pallas_call(
        functools.partial(_gla_prompt_kernel, mode=mode, layer=layer),
        out_shape=(jax.ShapeDtypeStruct((batch, seq, n_heads * LANES), BF16),
                   jax.ShapeDtypeStruct((batch, n_heads, HEAD_DIM, HEAD_DIM), F32)),
        grid=(batch, n_heads),
        in_specs=in_specs,
        out_specs=(pl.BlockSpec((None, seq, LANES), lambda b, h: (b, 0, h)),
                   pl.BlockSpec((None, None, HEAD_DIM, HEAD_DIM), lambda b, h: (b, h, 0, 0))),
        scratch_shapes=[pltpu.VMEM((HEAD_DIM, HEAD_DIM), F32)],
        compiler_params=_cparams(("parallel", "parallel")),
        name=mode + "_prompt",
    )(*args)


def _diag_extract(row, n_rows, offset):
    r = lax.broadcasted_iota(jnp.int32, (n_rows, LANES), 0)
    l = lax.broadcasted_iota(jnp.int32, (n_rows, LANES), 1)
    return jnp.sum(jnp.where(l == r + offset, jnp.broadcast_to(row, (n_rows, LANES)), 0.0), axis=1, keepdims=True)


def _columns(x, n_pad):
    n = x.shape[0]
    if n_pad > n:
        x = jnp.concatenate([x, jnp.zeros((n_pad - n, x.shape[1]), F32)], axis=0)
    return x.T


def _dn_decode_kernel(p_ref, sm_ref, conv_ref, s_ref, cw_ref, alog_ref, dtb_ref, gain_ref,
                      o_ref, conv_out_ref, s_out_ref, *, n_heads, slab_g, bb):
    nh = n_heads
    pad = -(-nh // SUBLANES) * SUBLANES
    cw = cw_ref[...]
    neg_a = -jnp.exp(alog_ref[...])
    dtb = dtb_ref[...]
    gain = gain_ref[...]
    for b in range(bb):
        x = p_ref[b, 0:3 * nh, :]
        y = cw[DN_CONV - 1] * x
        for i in range(DN_CONV - 1):
            y = y + cw[i] * conv_ref[b, i]
        for i in range(DN_CONV - 2):
            conv_out_ref[b, i] = conv_ref[b, i + 1]
        conv_out_ref[b, DN_CONV - 2] = x
        act = _silu(y)
        q = _l2norm(act[0:nh]) * (HEAD_DIM ** -0.5)
        k = _l2norm(act[nh:2 * nh])
        v = act[2 * nh:3 * nh]
        srow = sm_ref[b:b + 1, :]
        beta = _diag_extract(_sigmoid(srow), nh, 0)
        g = _diag_extract(neg_a * _softplus(srow + dtb), nh, nh)
        eg = jnp.exp(g)
        k_t = _columns(k, pad)
        q_t = _columns(q, pad)
        w_t = _columns(k * (beta * eg), pad)
        u = v * beta
        outs = []
        for h in range(nh):
            s = s_ref[b, h]
            v_new = u[h:h + 1] - jnp.sum(w_t[:, h:h + 1] * s, axis=0, keepdims=True)
            s_new = s * eg[h:h + 1] + k_t[:, h:h + 1] * v_new
            s_out_ref[b, h] = s_new
            outs.append(jnp.sum(q_t[:, h:h + 1] * s_new, axis=0, keepdims=True))
        o = jnp.concatenate(outs, axis=0)
        gate = p_ref[b, slab_g:slab_g + nh, :]
        o_ref[b] = _gated_head_norm(o, gate, gain).astype(o_ref.dtype)


def _dn_decode(proj, sm, conv_state, s_state, cw, alog_vec, dtb_vec, gain, *, n_heads, slab_g, bb):
    batch, n_slabs, _ = proj.shape
    vec_spec = pl.BlockSpec((1, LANES), lambda i: (0, 0))
    return pl.pallas_call(
        functools.partial(_dn_decode_kernel, n_heads=n_heads, slab_g=slab_g, bb=bb),
        out_shape=(jax.ShapeDtypeStruct((batch, n_heads, LANES), BF16),
                   jax.ShapeDtypeStruct(conv_state.shape, F32),
                   jax.ShapeDtypeStruct(s_state.shape, F32)),
        grid=(batch // bb,),
        in_specs=[
            pl.BlockSpec((bb, n_slabs, LANES), lambda i: (i, 0, 0)),
            pl.BlockSpec((bb, LANES), lambda i: (i, 0)),
            pl.BlockSpec((bb,) + conv_state.shape[1:], lambda i: (i, 0, 0, 0)),
            pl.BlockSpec((bb,) + s_state.shape[1:], lambda i: (i, 0, 0, 0)),
            pl.BlockSpec(cw.shape, lambda i: (0, 0, 0)),
            vec_spec, vec_spec, vec_spec,
        ],
        out_specs=(pl.BlockSpec((bb, n_heads, LANES), lambda i: (i, 0, 0)),
                   pl.BlockSpec((bb,) + conv_state.shape[1:], lambda i: (i, 0, 0, 0)),
                   pl.BlockSpec((bb,) + s_state.shape[1:], lambda i: (i, 0, 0, 0))),
        compiler_params=_cparams(("parallel",)),
        name="dn_decode",
    )(proj, sm, conv_state, s_state, cw, alog_vec, dtb_vec, gain)


def _gla_gate_decode_kernel(sm_ref, wg_ref, bias_ref, o_ref):
    z = _dot(sm_ref[...].astype(BF16), wg_ref[...]) + bias_ref[...]
    o_ref[...] = _log_sigmoid(z) * (1.0 / GLA_NORMALIZER)


def _gla_gate_decode(sm, wg_full, bias):
    m = sm.shape[0]
    n = wg_full.shape[1]
    return pl.pallas_call(
        _gla_gate_decode_kernel,
        out_shape=jax.ShapeDtypeStruct((m, n), F32),
        name="gla_gate_decode",
    )(sm, wg_full, bias)


def _gla_decode_kernel(*refs, mode, layer, n_heads, slabs, bb):
    if mode == "gla":
        p_ref, ga_ref, s_ref, gain_ref, o_ref, s_out_ref = refs
    else:
        p_ref, lbl_ref, s_ref, gain_ref, o_ref, s_out_ref = refs
    nh = n_heads
    pad = -(-nh // SUBLANES) * SUBLANES
    sq, sk, sv, sg = slabs
    gain = gain_ref[...]
    for b in range(bb):
        if mode == "gla":
            q = p_ref[b, sq:sq + nh, :] * (HEAD_DIM ** -0.5)
            k = p_ref[b, sk:sk + nh, :]
            g = ga_ref[b]
        else:
            q = _silu(p_ref[b, sq:sq + nh, :])
            lb = jnp.concatenate([_hg_lower_bound(lbl_ref[h], layer) for h in range(nh)], axis=0)
            g, k = _hg_gates(p_ref[b, sk:sk + nh, :], lb)
        v = p_ref[b, sv:sv + nh, :]
        a_t = _columns(jnp.exp(g), pad)
        k_t = _columns(k, pad)
        q_t = _columns(q, pad)
        outs = []
        for h in range(nh):
            s_new = a_t[:, h:h + 1] * s_ref[b, h] + k_t[:, h:h + 1] * v[h:h + 1]
            s_out_ref[b, h] = s_new
            outs.append(jnp.sum(q_t[:, h:h + 1] * s_new, axis=0, keepdims=True))
        o = jnp.concatenate(outs, axis=0)
        o_ref[b] = _gated_head_norm(o, p_ref[b, sg:sg + nh, :], gain).astype(o_ref.dtype)


def _gla_decode(proj, s_state, gain, *, mode, layer, n_heads, slabs, bb, ga=None, lbl=None):
    batch, n_slabs, _ = proj.shape
    if mode == "gla":
        extra, extra_spec = ga, pl.BlockSpec((bb, n_heads, LANES), lambda i: (i, 0, 0))
    else:
        extra, extra_spec = lbl, pl.BlockSpec(lbl.shape, lambda i: (0, 0, 0))
    return pl.pallas_call(
        functools.partial(_gla_decode_kernel, mode=mode, layer=layer, n_heads=n_heads, slabs=slabs, bb=bb),
        out_shape=(jax.ShapeDtypeStruct((batch, n_heads, LANES), BF16), jax.ShapeDtypeStruct(s_state.shape, F32)),
        grid=(batch // bb,),
        in_specs=[
            pl.BlockSpec((bb, n_slabs, LANES), lambda i: (i, 0, 0)),
            extra_spec,
            pl.BlockSpec((bb,) + s_state.shape[1:], lambda i: (i, 0, 0, 0)),
            pl.BlockSpec((1, LANES), lambda i: (0, 0)),
        ],
        out_specs=(pl.BlockSpec((bb, n_heads, LANES), lambda i: (i, 0, 0)),
                   pl.BlockSpec((bb,) + s_state.shape[1:], lambda i: (i, 0, 0, 0))),
        compiler_params=_cparams(("parallel",)),
        name=mode + "_decode",
    )(proj, extra, s_state, gain)


def _layer_weights(l, w_in, dn_conv_w, dn_a_log, dn_dt_bias, gla_w_gate, gla_gate_bias, w_out,
                   ffn_w_gate, ffn_w_up, ffn_conv_w, ffn_conv_b, ffn_w_down, dims):
    dn_h, gla_h, hg_h = dims["dn_h"], dims["gla_h"], dims["hg_h"]
    dn_dim, gla_dim, hg_dim = dn_h * HEAD_DIM, gla_h * HEAD_DIM, hg_h * HEAD_DIM
    sizes = (3 * dn_dim, dn_h, dn_h, dn_dim, gla_dim, gla_dim, gla_dim, GLA_LOWRANK, gla_dim,
             hg_dim, hg_dim, hg_dim, hg_dim)
    offs = [0]
    for s in sizes:
        offs.append(offs[-1] + s)
    wl = w_in[l]
    piece = lambda i: wl[:, offs[i]:offs[i + 1]]
    wide = [0, 3, 4, 5, 6, 8, 9, 10, 11, 12]
    w_big = jnp.concatenate([piece(i) for i in wide], axis=1).astype(BF16)
    small = jnp.concatenate([piece(1), piece(2), piece(7)], axis=1)
    n_small = small.shape[1]
    w_small = jnp.pad(small, ((0, 0), (0, LANES - n_small))).astype(BF16)

    d_ff = ffn_w_gate.shape[2]
    d_ffp = -(-d_ff // FF_ALIGN) * FF_ALIGN
    padc = ((0, 0), (0, d_ffp - d_ff))
    wo = w_out[l].astype(BF16)
    lr0 = 2 * dn_h
    wg_rows = jnp.pad(gla_w_gate[l], ((lr0, LANES - lr0 - GLA_LOWRANK), (0, 0))).astype(BF16)
    return dict(
        w_big=w_big, w_small=w_small,
        wo=(wo[:dn_dim], wo[dn_dim:dn_dim + gla_dim], wo[dn_dim + gla_dim:]),
        dn_cw=dn_conv_w[l].reshape(DN_CONV, 3 * dn_h, HEAD_DIM),
        alog_vec=jnp.pad(dn_a_log[l], (dn_h, LANES - 2 * dn_h)).reshape(1, LANES),
        dtb_vec=jnp.pad(dn_dt_bias[l], (dn_h, LANES - 2 * dn_h)).reshape(1, LANES),
        gla_wg_full=wg_rows,
        gla_wg_heads=wg_rows.reshape(LANES, gla_h, HEAD_DIM).transpose(1, 0, 2),
        gla_bias=gla_gate_bias[l].reshape(1, gla_dim),
        ffn_wg=jnp.pad(ffn_w_gate[l], padc).astype(BF16),
        ffn_wu=jnp.pad(ffn_w_up[l], padc).astype(BF16),
        ffn_wd=jnp.pad(ffn_w_down[l], ((0, d_ffp - d_ff), (0, 0))).astype(BF16),
        ffn_cw=jnp.pad(ffn_conv_w[l], padc),
        ffn_cb=jnp.pad(ffn_conv_b[l], (0, d_ffp - d_ff)).reshape(1, d_ffp),
        d_ff=d_ff, d_ffp=d_ffp,
    )


def _slab_table(dims):
    dn_h, gla_h, hg_h = dims["dn_h"], dims["gla_h"], dims["hg_h"]
    names = [("dn_qkv", 3 * dn_h), ("dn_g", dn_h), ("gla_q", gla_h), ("gla_k", gla_h), ("gla_v", gla_h),
             ("gla_g", gla_h), ("hg_q", hg_h), ("hg_f", hg_h), ("hg_i", hg_h), ("hg_g", hg_h)]
    table, pos = {}, 0
    for n, cnt in names:
        table[n] = pos
        pos += cnt
    return table


def _prompt_layer(x, l, lw, norm_mix, dn_norm, gla_norm, hg_norm, hg_lbl, norm_ffn, dims, batch):
    m, d = x.shape
    seq = m // batch
    dn_h, gla_h, hg_h = dims["dn_h"], dims["gla_h"], dims["hg_h"]
    sl = _slab_table(dims)
    h = _rmsnorm(x, norm_mix[l], BF16)
    proj = _matmul([h], [lw["w_big"]], slab=True, tm=1024, tn=1024, name="in_proj")
    sm = _matmul([h], [lw["w_small"]], tm=1024, tn=LANES, name="in_proj_small")
    proj4 = proj.reshape(proj.shape[0], batch, seq, LANES)
    sm3 = sm.reshape(batch, seq, LANES)
    cw = lw["dn_cw"].transpose(1, 0, 2)
    o_dn, s_dn = _dn_prompt(proj4, sm3, cw, lw["alog_vec"], lw["dtb_vec"], dn_norm[l].reshape(1, LANES),
                            batch=batch, n_heads=dn_h, slab_q=sl["dn_qkv"], slab_g=sl["dn_g"])
    o_gla, s_gla = _gla_prompt(proj4, mode="gla", layer=l, batch=batch, n_heads=gla_h,
                               slabs=(sl["gla_q"], sl["gla_k"], sl["gla_v"], sl["gla_g"]),
                               sm=sm3, wg=lw["gla_wg_heads"], bias=lw["gla_bias"].reshape(gla_h, 1, LANES),
                               gain=gla_norm[l].reshape(1, LANES))
    o_hg, s_hg = _gla_prompt(proj4, mode="hg", layer=l, batch=batch, n_heads=hg_h,
                             slabs=(sl["hg_q"], sl["hg_f"], sl["hg_i"], sl["hg_g"]),
                             lbl=hg_lbl, gain=hg_norm[l].reshape(1, LANES))
    o_list = [o_dn.reshape(m, -1), o_gla.reshape(m, -1), o_hg.reshape(m, -1)]
    x = _matmul(o_list, list(lw["wo"]), res=x, tm=1024, tn=512, name="out_proj")
    h2 = _rmsnorm(x, norm_ffn[l], BF16)
    hidden, tail = _ffn_up_prompt(h2, lw["ffn_wg"], lw["ffn_wu"], lw["ffn_cw"], lw["ffn_cb"],
                                  batch=batch, tm=512, tn=512)
    x = _matmul_ktiled(hidden, lw["ffn_wd"], x, tm=1024, tn=1024, tk=lw["d_ffp"] // 4, name="ffn_down")
    n_qkv = 3 * dn_h
    conv_dn = proj4[sl["dn_qkv"]:sl["dn_qkv"] + n_qkv, :, seq - (DN_CONV - 1):, :]
    conv_dn = conv_dn.transpose(1, 2, 0, 3).reshape(batch, DN_CONV - 1, n_qkv * LANES)
    conv_ffn = tail[:, SUBLANES - (FFN_CONV - 1):, :lw["d_ff"]]
    return x, (conv_dn, s_dn, s_gla, s_hg, conv_ffn)


def _decode_layer(x, l, lw, st_dn_conv, st_dn, st_gla, st_hg, st_ffn, norm_mix, dn_norm, gla_norm, hg_norm,
                  hg_lbl, norm_ffn, dims):
    m, d = x.shape
    dn_h, gla_h, hg_h = dims["dn_h"], dims["gla_h"], dims["hg_h"]
    sl = _slab_table(dims)
    bb = SUBLANES
    h = _rmsnorm(x, norm_mix[l], BF16)
    proj = _matmul([h], [lw["w_big"]], tm=m, tn=1024, name="in_proj_dec")
    sm = _matmul([h], [lw["w_small"]], tm=m, tn=LANES, name="in_proj_small_dec")
    proj3 = proj.reshape(m, -1, LANES)
    conv_in = st_dn_conv[l].reshape(m, DN_CONV - 1, 3 * dn_h, LANES)
    o_dn, conv_dn, s_dn = _dn_decode(proj3, sm, conv_in, st_dn[l], lw["dn_cw"], lw["alog_vec"], lw["dtb_vec"],
                                     dn_norm[l].reshape(1, LANES), n_heads=dn_h, slab_g=sl["dn_g"], bb=bb)
    ga = _gla_gate_decode(sm, lw["gla_wg_full"], lw["gla_bias"]).reshape(m, gla_h, LANES)
    o_gla, s_gla = _gla_decode(proj3, st_gla[l], gla_norm[l].reshape(1, LANES), mode="gla", layer=l, n_heads=gla_h,
                               slabs=(sl["gla_q"], sl["gla_k"], sl["gla_v"], sl["gla_g"]), bb=bb, ga=ga)
    o_hg, s_hg = _gla_decode(proj3, st_hg[l], hg_norm[l].reshape(1, LANES), mode="hg", layer=l, n_heads=hg_h,
                             slabs=(sl["hg_q"], sl["hg_f"], sl["hg_i"], sl["hg_g"]), bb=bb, lbl=hg_lbl)
    o_list = [o_dn.reshape(m, -1), o_gla.reshape(m, -1), o_hg.reshape(m, -1)]
    x = _matmul(o_list, list(lw["wo"]), res=x, tm=m, tn=1024, name="out_proj_dec")
    h2 = _rmsnorm(x, norm_ffn[l], BF16)
    d_ff, d_ffp = lw["d_ff"], lw["d_ffp"]
    st2d = jnp.pad(st_ffn[l], ((0, 0), (0, 0), (0, d_ffp - d_ff))).reshape(m, (FFN_CONV - 1) * d_ffp)
    hidden, gate = _ffn_up_decode(h2, lw["ffn_wg"], lw["ffn_wu"], st2d, lw["ffn_cw"], lw["ffn_cb"], tn=512)
    x = _matmul_ktiled(hidden, lw["ffn_wd"], x, tm=m, tn=1024, tk=d_ffp // 4, name="ffn_down_dec")
    conv_ffn = jnp.stack([st_ffn[l][:, 1, :], gate[:, :d_ff]], axis=1)
    return x, (conv_dn.reshape(m, DN_CONV - 1, -1), s_dn, s_gla, s_hg, conv_ffn)


def kernel(x_prompt, x_sample, state_dn_conv, state_dn, state_gla, state_hg, state_ffn_conv, norm_mix, w_in, dn_conv_w, dn_a_log, dn_dt_bias, dn_norm, gla_w_gate, gla_gate_bias, gla_norm, hg_lb_logits, hg_norm, w_out, norm_ffn, ffn_w_gate, ffn_w_up, ffn_conv_w, ffn_conv_b, ffn_w_down, norm_final):
    batch, seq, d_model = x_prompt.shape
    dec_batch, dec_seq, _ = x_sample.shape
    assert dec_seq == 1 and FFN_CONV == 3
    depth = w_in.shape[0]
    dims = dict(dn_h=state_dn.shape[2], gla_h=state_gla.shape[2], hg_h=state_hg.shape[2])
    hg_lbl = hg_lb_logits.reshape(depth, dims["hg_h"], HEAD_DIM).transpose(1, 0, 2)

    xp = x_prompt.reshape(batch * seq, d_model)
    xs = x_sample.reshape(dec_batch, d_model)
    p_states, s_states = [], []
    for l in range(depth):
        lw = _layer_weights(l, w_in, dn_conv_w, dn_a_log, dn_dt_bias, gla_w_gate, gla_gate_bias, w_out,
                            ffn_w_gate, ffn_w_up, ffn_conv_w, ffn_conv_b, ffn_w_down, dims)
        xp, ps = _prompt_layer(xp, l, lw, norm_mix, dn_norm, gla_norm, hg_norm, hg_lbl, norm_ffn, dims, batch)
        xs, ss = _decode_layer(xs, l, lw, state_dn_conv, state_dn, state_gla, state_hg, state_ffn_conv,
                               norm_mix, dn_norm, gla_norm, hg_norm, hg_lbl, norm_ffn, dims)
        p_states.append(ps)
        s_states.append(ss)
    y_prompt = _rmsnorm(xp, norm_final, F32).reshape(batch, seq, d_model)
    y_sample = _rmsnorm(xs, norm_final, F32).reshape(dec_batch, dec_seq, d_model)
    stack = lambda states, i: jnp.stack([s[i] for s in states], axis=0)
    return ((y_prompt, y_sample) + tuple(stack(p_states, i) for i in range(5))
            + tuple(stack(s_states, i) for i in range(5)))
```

```python
import functools
import math

import jax
import jax.numpy as jnp
from jax import lax
from jax.experimental import pallas as pl
from jax.experimental.pallas import tpu as pltpu

F32 = jnp.float32
BF16 = jnp.bfloat16

HEAD_DIM = 128
DN_CONV = 4
GLA_LOWRANK = 16
GLA_NORMALIZER = 16.0
FFN_CONV = 3
EPS = 1e-6
LB_FLOOR = 1e-30

LANES = 128
SUBLANES = 8
BF16_SUBLANES = 16
MXU_DIM = 256
VMEM_LIMIT_BYTES = 56 * 1024 * 1024

CHUNK = 64
SUB = 8


def _cparams(sem):
    return pltpu.CompilerParams(dimension_semantics=sem, vmem_limit_bytes=VMEM_LIMIT_BYTES)


def _sigmoid(x):
    return 1.0 / (1.0 + jnp.exp(-x))


def _silu(x):
    return x * _sigmoid(x)


def _softplus(x):
    return jnp.maximum(x, 0.0) + jnp.log1p(jnp.exp(-jnp.abs(x)))


def _log_sigmoid(x):
    return -_softplus(-x)


def _logaddexp(a, b):
    return jnp.maximum(a, b) + jnp.log1p(jnp.exp(-jnp.abs(a - b)))


def _dot(a, b):
    return jnp.dot(a, b, preferred_element_type=F32)


def _dot_nt(a, b, precision=None):
    return lax.dot_general(a, b, (((1,), (1,)), ((), ())), preferred_element_type=F32, precision=precision)


def _dot_tn(a, b):
    return lax.dot_general(a, b, (((0,), (0,)), ((), ())), preferred_element_type=F32)


def _tri_incl(n):
    r = lax.broadcasted_iota(jnp.int32, (n, n), 0)
    c = lax.broadcasted_iota(jnp.int32, (n, n), 1)
    return r, c


def _gated_head_norm(o, gate, gain):
    o = o * lax.rsqrt(jnp.mean(o * o, axis=-1, keepdims=True) + EPS) * gain
    return o * _silu(gate)


def _rmsnorm_kernel(x_ref, w_ref, o_ref):
    x = x_ref[...]
    y = x * lax.rsqrt(jnp.mean(x * x, axis=-1, keepdims=True) + EPS)
    o_ref[...] = (y * w_ref[...]).astype(o_ref.dtype)


def _rmsnorm(x, w, out_dtype):
    m, d = x.shape
    tm = min(m, 256)
    return pl.pallas_call(
        _rmsnorm_kernel,
        out_shape=jax.ShapeDtypeStruct((m, d), out_dtype),
        grid=(m // tm,),
        in_specs=[pl.BlockSpec((tm, d), lambda i: (i, 0)), pl.BlockSpec((1, d), lambda i: (0, 0))],
        out_specs=pl.BlockSpec((tm, d), lambda i: (i, 0)),
        compiler_params=_cparams(("parallel",)),
        name="rmsnorm",
    )(x, w.reshape(1, d))


def _mm_fullk_kernel(*refs, n_lhs, has_res, slab):
    lhs = refs[:n_lhs]
    rhs = refs[n_lhs:2 * n_lhs]
    res_ref = refs[2 * n_lhs] if has_res else None
    o_ref = refs[2 * n_lhs + (1 if has_res else 0)]
    acc = _dot(lhs[0][...], rhs[0][...])
    for a, b in zip(lhs[1:], rhs[1:]):
        acc = acc + _dot(a[...], b[...])
    if has_res:
        acc = res_ref[...] + acc
    if slab:
        for j in range(o_ref.shape[0]):
            o_ref[j] = acc[:, j * LANES:(j + 1) * LANES]
    else:
        o_ref[...] = acc.astype(o_ref.dtype)


def _matmul(lhs_list, rhs_list, *, res=None, slab=False, tm, tn, name, layer=None):
    m = lhs_list[0].shape[0]
    n = rhs_list[0].shape[-1]
    tm = min(tm, m)
    tn = min(tn, n)
    in_specs = [pl.BlockSpec((tm, a.shape[1]), lambda i, j: (i, 0)) for a in lhs_list]
    for b in rhs_list:
        if b.ndim == 3:
            in_specs.append(pl.BlockSpec((None, b.shape[1], tn), lambda i, j: (layer, 0, j)))
        else:
            in_specs.append(pl.BlockSpec((b.shape[0], tn), lambda i, j: (0, j)))
    args = list(lhs_list) + list(rhs_list)
    if res is not None:
        in_specs.append(pl.BlockSpec((tm, tn), lambda i, j: (i, j)))
        args.append(res)
    if slab:
        out_shape = jax.ShapeDtypeStruct((n // LANES, m, LANES), F32)
        out_spec = pl.BlockSpec((tn // LANES, tm, LANES), lambda i, j: (j, i, 0))
    else:
        out_shape = jax.ShapeDtypeStruct((m, n), F32)
        out_spec = pl.BlockSpec((tm, tn), lambda i, j: (i, j))
    return pl.pallas_call(
        functools.partial(_mm_fullk_kernel, n_lhs=len(lhs_list), has_res=res is not None, slab=slab),
        out_shape=out_shape,
        grid=(m // tm, n // tn),
        in_specs=in_specs,
        out_specs=out_spec,
        compiler_params=_cparams(("parallel", "arbitrary")),
        name=name,
    )(*args)


def _ffn_up_prompt_kernel(x_ref, wg_ref, wu_ref, cw_ref, cb_ref, hid_ref, tail_ref):
    x = x_ref[...]
    gate = _dot(x, wg_ref[...])
    up = _dot(x, wu_ref[...])
    rows = gate.shape[0]
    cat = jnp.concatenate([jnp.zeros((SUBLANES, gate.shape[1]), F32), gate], axis=0)
    cw = cw_ref[...]
    y = cw[FFN_CONV - 1:FFN_CONV] * gate
    for i in range(FFN_CONV - 1):
        back = FFN_CONV - 1 - i
        y = y + cw[i:i + 1] * cat[SUBLANES - back:SUBLANES - back + rows]
    hid_ref[...] = (_silu(y + cb_ref[...]) * up).astype(hid_ref.dtype)
    tail_ref[...] = gate[rows - SUBLANES:]


def _ffn_up_prompt(h, wg, wu, cw, cb, *, layer, batch, tn):
    m, d = h.shape
    n = wg.shape[2]
    seq = m // batch
    w_spec = pl.BlockSpec((None, d, tn), lambda b, j: (layer, 0, j))
    return pl.pallas_call(
        _ffn_up_prompt_kernel,
        out_shape=(jax.ShapeDtypeStruct((m, n), BF16), jax.ShapeDtypeStruct((batch, SUBLANES, n), F32)),
        grid=(batch, n // tn),
        in_specs=[
            pl.BlockSpec((seq, d), lambda b, j: (b, 0), pipeline_mode=pl.Buffered(1)),
            w_spec, w_spec,
            pl.BlockSpec((None, FFN_CONV, tn), lambda b, j: (layer, 0, j)),
            pl.BlockSpec((None, 1, tn), lambda b, j: (layer, 0, j)),
        ],
        out_specs=(
            pl.BlockSpec((seq, tn), lambda b, j: (b, j)),
            pl.BlockSpec((None, SUBLANES, tn), lambda b, j: (b, 0, j)),
        ),
        compiler_params=_cparams(("parallel", "parallel")),
        name="ffn_up_prompt",
    )(h, wg, wu, cw, cb)


def _ffn_up_decode_kernel(x_ref, wg_ref, wu_ref, s0_ref, s1_ref, cw_ref, cb_ref, hid_ref, gate_ref):
    x = x_ref[...]
    gate = _dot(x, wg_ref[...])
    up = _dot(x, wu_ref[...])
    cw = cw_ref[...]
    y = cw[0:1] * s0_ref[...] + cw[1:2] * s1_ref[...] + cw[2:3] * gate
    hid_ref[...] = (_silu(y + cb_ref[...]) * up).astype(hid_ref.dtype)
    gate_ref[...] = gate


def _ffn_up_decode(h, wg, wu, state, cw, cb, *, layer, tn):
    m, d = h.shape
    n = wg.shape[2]
    nt = n // tn
    w_spec = pl.BlockSpec((None, d, tn), lambda j: (layer, 0, j))
    return pl.pallas_call(
        _ffn_up_decode_kernel,
        out_shape=(jax.ShapeDtypeStruct((m, n), BF16), jax.ShapeDtypeStruct((m, n), F32)),
        grid=(nt,),
        in_specs=[
            pl.BlockSpec((m, d), lambda j: (0, 0)),
            w_spec, w_spec,
            pl.BlockSpec((None, m, tn), lambda j: (layer, 0, j)),
            pl.BlockSpec((None, m, tn), lambda j: (layer, 0, nt + j)),
            pl.BlockSpec((None, FFN_CONV, tn), lambda j: (layer, 0, j)),
            pl.BlockSpec((None, 1, tn), lambda j: (layer, 0, j)),
        ],
        out_specs=(pl.BlockSpec((m, tn), lambda j: (0, j)), pl.BlockSpec((m, tn), lambda j: (0, j))),
        compiler_params=_cparams(("parallel",)),
        name="ffn_up_decode",
    )(h, wg, wu, state, state, cw, cb)


def _lane_select(x, lane_index):
    lane = lax.broadcasted_iota(jnp.int32, x.shape, 1)
    return jnp.sum(jnp.where(lane == lane_index, x, 0.0), axis=1, keepdims=True)


def _causal_conv_chunk(x_ref, cw, c, rows):
    start = pl.multiple_of(c * rows, rows)
    xc = x_ref[pl.ds(start, rows), :]
    pstart = pl.multiple_of(jnp.maximum(c * rows - SUBLANES, 0), SUBLANES)
    xp = x_ref[pl.ds(pstart, SUBLANES), :]
    xp = jnp.where(c == 0, 0.0, xp)
    cat = jnp.concatenate([xp, xc], axis=0)
    y = cw[DN_CONV - 1:DN_CONV] * xc
    for i in range(DN_CONV - 1):
        back = DN_CONV - 1 - i
        y = y + cw[i:i + 1] * cat[SUBLANES - back:SUBLANES - back + rows]
    return y


def _l2norm(x):
    return x * lax.rsqrt(jnp.sum(x * x, axis=-1, keepdims=True) + EPS)


def _split2(x):
    hi = x.astype(BF16)
    lo = (x - hi.astype(F32)).astype(BF16)
    return hi, lo


def _split3(x):
    hi = x.astype(BF16)
    r1 = x - hi.astype(F32)
    mid = r1.astype(BF16)
    lo = (r1 - mid.astype(F32)).astype(BF16)
    return hi, mid, lo


def _dot3(a, b, dot=_dot):
    a_hi, a_lo = _split2(a)
    b_hi, b_lo = _split2(b)
    return dot(a_hi, b_hi) + (dot(a_hi, b_lo) + dot(a_lo, b_hi))


def _dot2(a, b):
    b_hi, b_lo = _split2(b)
    a_b = a.astype(BF16)
    return _dot(a_b, b_hi) + _dot(a_b, b_lo)


def _cumsum_rows(tri_bf16, g):
    hi, mid, lo = _split3(g)
    return _dot(tri_bf16, hi) + (_dot(tri_bf16, mid) + _dot(tri_bf16, lo))


def _unit_lower_solve(mms, rhss):
    n = mms[0].shape[0]
    xs = [-mm for mm in mms]
    rs = [rhs + _dot2(x, rhs) for x, rhs in zip(xs, rhss)]
    for _ in range(int(math.log2(n)) - 1):
        xs = [_dot2(x, x) for x in xs]
        rs = [r + _dot2(x, r) for x, r in zip(xs, rs)]
    return rs


DN_HEADS_PER_STEP = 2
DN_CHUNKS_PER_ITER = 4


def _dn_prompt_kernel(q_ref, k_ref, v_ref, gt_ref, sm_ref, cwq_ref, cwk_ref, cwv_ref, alog_ref, dtb_ref, gain_ref,
                      o_ref, s_ref, u_s, w_s, qe_s, kd_s, attn_s, gend_s, st_s, *, n_heads):
    hp_n = q_ref.shape[0]
    head0 = pl.program_id(1) * hp_n
    seq = q_ref.shape[1]
    n_chunks = seq // CHUNK
    r, c = _tri_incl(CHUNK)
    tri = (r >= c).astype(BF16)
    incl = r >= c
    strict = r > c
    neg_a = -jnp.exp(alog_ref[...])
    dtb = dtb_ref[...]

    def phase1(it, carry):
        items = [(hp, it * DN_CHUNKS_PER_ITER + u) for hp in range(hp_n) for u in range(DN_CHUNKS_PER_ITER)]
        pre = []
        for hp, ci in items:
            rows = pl.ds(pl.multiple_of(ci * CHUNK, CHUNK), CHUNK)
            q = _silu(_causal_conv_chunk(q_ref.at[hp], cwq_ref[hp], ci, CHUNK))
            k = _silu(_causal_conv_chunk(k_ref.at[hp], cwk_ref[hp], ci, CHUNK))
            v = _silu(_causal_conv_chunk(v_ref.at[hp], cwv_ref[hp], ci, CHUNK))
            q = _l2norm(q) * (HEAD_DIM ** -0.5)
            k = _l2norm(k)
            sm = sm_ref[rows, :]
            beta = _lane_select(_sigmoid(sm), head0 + hp)
            g = _lane_select(neg_a * _softplus(sm + dtb), n_heads + head0 + hp)
            pre.append((q, k, v, beta, jnp.broadcast_to(g, (CHUNK, LANES))))
        big_gs = [_cumsum_rows(tri, p[4]) for p in pre]
        mms, rhss, gammas, egs = [], [], [], []
        for (q, k, v, beta, _), big_g in zip(pre, big_gs):
            g_row = big_g.T[:CHUNK, :]
            g_col = big_g[:, :CHUNK]
            gamma = jnp.where(incl, jnp.exp(jnp.where(incl, g_col - g_row, 0.0)), 0.0)
            kb = k.astype(BF16)
            mms.append(jnp.where(strict, beta * _dot_nt(kb, kb) * gamma, 0.0))
            eg = jnp.exp(big_g)
            rhss.append(jnp.concatenate([v * beta, k * (beta * eg)], axis=1))
            gammas.append(gamma)
            egs.append(eg)
        uws = _unit_lower_solve(mms, rhss)
        for (hp, ci), (q, k, v, beta, _), big_g, gamma, eg, uw in zip(items, pre, big_gs, gammas, egs, uws):
            rows = pl.ds(pl.multiple_of(ci * CHUNK, CHUNK), CHUNK)
            g_end = big_g[CHUNK - 1:CHUNK, :]
            u_s[hp, rows, :] = uw[:, :HEAD_DIM]
            w_s[hp, rows, :] = uw[:, HEAD_DIM:]
            qe_s[hp, rows, :] = q * eg
            kd_s[hp, rows, :] = k * jnp.exp(g_end - big_g)
            attn_s[hp, ci] = _dot_nt(q.astype(BF16), k.astype(BF16)) * gamma
            gend_s[hp, ci] = jnp.broadcast_to(jnp.exp(g_end), (SUBLANES, LANES))
        return carry

    lax.fori_loop(0, n_chunks // DN_CHUNKS_PER_ITER, phase1, 0)

    st_s[...] = jnp.zeros_like(st_s)
    gain = gain_ref[...]

    def phase2(ci, carry):
        rows = pl.ds(pl.multiple_of(ci * CHUNK, CHUNK), CHUNK)
        ss = [st_s[hp] for hp in range(hp_n)]
        sbs = [s.astype(BF16) for s in ss]
        vbs = [(u_s[hp, rows, :] - _dot(w_s[hp, rows, :].astype(BF16), sbs[hp])).astype(BF16) for hp in range(hp_n)]
        for hp in range(hp_n):
            o = _dot(qe_s[hp, rows, :].astype(BF16), sbs[hp]) + _dot(attn_s[hp, ci].astype(BF16), vbs[hp])
            st_s[hp] = ss[hp] * gend_s[hp, ci][0:1, :] + _dot_tn(kd_s[hp, rows, :].astype(BF16), vbs[hp])
            o_ref[rows, hp * LANES:(hp + 1) * LANES] = _gated_head_norm(o, gt_ref[hp, rows, :], gain).astype(o_ref.dtype)
        return carry

    lax.fori_loop(0, n_chunks, phase2, 0)
    s_ref[...] = st_s[...]


def _dn_prompt(proj, sm, cw, alog_vec, dtb_vec, gain, *, batch, n_heads, slab_q, slab_g):
    seq = proj.shape[2]
    n_chunks = seq // CHUNK
    hp = math.gcd(DN_HEADS_PER_STEP, n_heads)
    assert slab_q % hp == 0 and slab_g % hp == 0 and n_chunks % DN_CHUNKS_PER_ITER == 0

    def slab_spec(first):
        return pl.BlockSpec((hp, None, seq, LANES), lambda b, j: (first // hp + j, b, 0, 0))

    def cw_spec(first):
        return pl.BlockSpec((hp, DN_CONV, LANES), lambda b, j: (first // hp + j, 0, 0))

    vec_spec = pl.BlockSpec((1, LANES), lambda b, j: (0, 0))
    return pl.pallas_call(
        functools.partial(_dn_prompt_kernel, n_heads=n_heads),
        out_shape=(jax.ShapeDtypeStruct((batch, seq, n_heads * LANES), BF16),
                   jax.ShapeDtypeStruct((batch, n_heads, HEAD_DIM, HEAD_DIM), F32)),
        grid=(batch, n_heads // hp),
        in_specs=[
            slab_spec(slab_q), slab_spec(slab_q + n_heads), slab_spec(slab_q + 2 * n_heads), slab_spec(slab_g),
            pl.BlockSpec((None, seq, LANES), lambda b, j: (b, 0, 0)),
            cw_spec(0), cw_spec(n_heads), cw_spec(2 * n_heads),
            vec_spec, vec_spec, vec_spec,
        ],
        out_specs=(pl.BlockSpec((None, seq, hp * LANES), lambda b, j: (b, 0, j)),
                   pl.BlockSpec((None, hp, HEAD_DIM, HEAD_DIM), lambda b, j: (b, j, 0, 0))),
        scratch_shapes=[pltpu.VMEM((hp, seq, LANES), F32)] * 4 + [
            pltpu.VMEM((hp, n_chunks, CHUNK, CHUNK), F32),
            pltpu.VMEM((hp, n_chunks, SUBLANES, LANES), F32),
            pltpu.VMEM((hp, HEAD_DIM, HEAD_DIM), F32),
        ],
        compiler_params=_cparams(("parallel", "parallel")),
        name="dn_prompt",
    )(proj, proj, proj, proj, sm, cw, cw, cw, alog_vec, dtb_vec, gain)


def _hg_lower_bound(logits, layer):
    mx = jnp.max(logits, axis=0, keepdims=True)
    e = jnp.exp(logits - mx)
    p = e / jnp.sum(e, axis=0, keepdims=True)
    lb = jnp.zeros((1, logits.shape[1]), F32)
    for i in range(1, layer + 1):
        lb = lb + p[i:i + 1]
    return lb


def _hg_gates(zf, lb):
    e = jnp.exp(-jnp.abs(zf))
    log_sig = jnp.minimum(zf, 0.0) - jnp.log1p(e)
    log_f = _logaddexp(jnp.log(jnp.maximum(lb, LB_FLOOR)), jnp.log1p(-lb) + log_sig)
    k = (1.0 - lb) * (jnp.where(zf >= 0.0, e, 1.0) / (1.0 + e))
    return log_f, k


def _gla_intra_off(q, k, big_g):
    off = []
    for lo in range(SUB, q.shape[0], SUB):
        ref = big_g[lo - 1:lo]
        qt = q[lo:lo + SUB] * jnp.exp(big_g[lo:lo + SUB] - ref)
        kt = k[:lo] * jnp.exp(ref - big_g[:lo])
        off.append(_dot3(qt, kt, _dot_nt))
    return off


def _gla_intra_diag(q, k, big_g):
    n = q.shape[0]
    lane = lax.broadcasted_iota(jnp.int32, (SUB, n), 1)
    diag = []
    for lo in range(0, n, SUB):
        qi = q[lo:lo + SUB]
        gi = big_g[lo:lo + SUB]
        ki = k[lo:lo + SUB]
        blk = jnp.zeros((SUB, n), F32)
        for j in range(SUB):
            dec = jnp.exp(jnp.minimum(gi - gi[j:j + 1], 0.0))
            col = jnp.sum(qi * ki[j:j + 1] * dec, axis=1, keepdims=True)
            blk = jnp.where(lane == lo + j, col, blk)
        diag.append(blk)
    return diag


def _gla_intra_merge(off, diag):
    n = diag[0].shape[1]
    rows = [diag[0]]
    for bi, o in enumerate(off, start=1):
        rows.append(diag[bi] + jnp.concatenate([o, jnp.zeros((SUB, n - bi * SUB), F32)], axis=1))
    r, c = _tri_incl(n)
    return jnp.where(r >= c, jnp.concatenate(rows, axis=0), 0.0)


GLA_CHUNKS_PER_ITER = 8


def _gla_prompt_kernel(*refs, mode, layer):
    if mode == "gla":
        (q_ref, k_ref, v_ref, gt_ref, sm_ref, wg_ref, bias_ref, gain_ref, o_ref, s_ref,
         st_s, a_s, qe_s, kd_s, de_s) = refs
    else:
        (q_ref, k_ref, v_ref, gt_ref, lbl_ref, gain_ref, o_ref, s_ref, st_s, a_s, qe_s, kd_s, de_s) = refs
        lb = _hg_lower_bound(lbl_ref[...], layer)
    seq = q_ref.shape[0]
    n_groups = seq // (CHUNK * GLA_CHUNKS_PER_ITER)
    r, c = _tri_incl(CHUNK)
    tri = (r >= c).astype(BF16)
    gain = gain_ref[...]
    st_s[...] = jnp.zeros_like(st_s)

    def group_rows(it):
        return [pl.ds(pl.multiple_of((it * GLA_CHUNKS_PER_ITER + u) * CHUNK, CHUNK), CHUNK)
                for u in range(GLA_CHUNKS_PER_ITER)]

    def front(it):
        all_rows, pre = group_rows(it), []
        for rows in all_rows:
            if mode == "gla":
                q = q_ref[rows, :] * (HEAD_DIM ** -0.5)
                k = k_ref[rows, :]
                z = _dot(sm_ref[rows, :].astype(BF16), wg_ref[...]) + bias_ref[...]
                g = _log_sigmoid(z) * (1.0 / GLA_NORMALIZER)
            else:
                q = _silu(q_ref[rows, :])
                g, k = _hg_gates(k_ref[rows, :], lb)
            pre.append((q, k, g))
        big_gs = [_cumsum_rows(tri, g) for _, _, g in pre]
        offs = [_gla_intra_off(q, k, big_g) for (q, k, _), big_g in zip(pre, big_gs)]
        diags = [_gla_intra_diag(q, k, big_g) for (q, k, _), big_g in zip(pre, big_gs)]
        for u, ((q, k, _), big_g, off, diag) in enumerate(zip(pre, big_gs, offs, diags)):
            g_end = big_g[CHUNK - 1:CHUNK, :]
            a_s[u] = _gla_intra_merge(off, diag).astype(BF16)
            qe_s[u] = (q * jnp.exp(big_g)).astype(BF16)
            kd_s[u] = (k * jnp.exp(g_end - big_g)).astype(BF16)
            de_s[u] = jnp.broadcast_to(jnp.exp(g_end), (SUBLANES, LANES))

    def back(it):
        all_rows = group_rows(it)
        vbs = [v_ref[rows, :].astype(BF16) for rows in all_rows]
        intra = [_dot(a_s[u], vb) for u, vb in enumerate(vbs)]
        incr = [_dot_tn(vb, kd_s[u]) for u, vb in enumerate(vbs)]
        states = [st_s[...]]
        for u, t in enumerate(incr):
            states.append(states[-1] * de_s[u][0:1, :] + t)
        for u, (rows, st, av) in enumerate(zip(all_rows, states, intra)):
            o = _dot_nt(qe_s[u], st.astype(BF16)) + av
            o_ref[rows, :] = _gated_head_norm(o, gt_ref[rows, :], gain).astype(o_ref.dtype)
        st_s[...] = states[-1]

    front(0)

    def body(it, carry):
        back(it - 1)
        front(it)
        return carry

    lax.fori_loop(1, n_groups, body, 0)
    back(n_groups - 1)
    s_ref[...] = st_s[...].T


def _gla_prompt(proj, *, mode, layer, batch, n_heads, slabs, sm=None, wg=None, bias=None, lbl=None, gain=None):
    seq = proj.shape[2]

    def slab_spec(first):
        return pl.BlockSpec((None, None, seq, LANES), lambda b, h: (first + h, b, 0, 0))

    in_specs = [slab_spec(s) for s in slabs]
    args = [proj] * 4
    if mode == "gla":
        in_specs += [
            pl.BlockSpec((None, seq, LANES), lambda b, h: (b, 0, 0)),
            pl.BlockSpec((None, LANES, LANES), lambda b, h: (h, 0, 0)),
            pl.BlockSpec((None, 1, LANES), lambda b, h: (h, 0, 0)),
        ]
        args += [sm, wg, bias]
    else:
        in_specs += [pl.BlockSpec((None, lbl.shape[1], LANES), lambda b, h: (h, 0, 0))]
        args += [lbl]
    in_specs.append(pl.BlockSpec((1, LANES), lambda b, h: (0, 0)))
    args.append(gain)
    return pl.pallas_call(
        functools.partial(_gla_prompt_kernel, mode=mode, layer=layer),
        out_shape=(jax.ShapeDtypeStruct((batch, seq, n_heads * LANES), BF16),
                   jax.ShapeDtypeStruct((batch, n_heads, HEAD_DIM, HEAD_DIM), F32)),
        grid=(batch, n_heads),
        in_specs=in_specs,
        out_specs=(pl.BlockSpec((None, seq, LANES), lambda b, h: (b, 0, h)),
                   pl.BlockSpec((None, None, HEAD_DIM, HEAD_DIM), lambda b, h: (b, h, 0, 0))),
        scratch_shapes=[
            pltpu.VMEM((HEAD_DIM, HEAD_DIM), F32),
            pltpu.VMEM((GLA_CHUNKS_PER_ITER, CHUNK, CHUNK), BF16),
            pltpu.VMEM((GLA_CHUNKS_PER_ITER, CHUNK, HEAD_DIM), BF16),
            pltpu.VMEM((GLA_CHUNKS_PER_ITER, CHUNK, HEAD_DIM), BF16),
            pltpu.VMEM((GLA_CHUNKS_PER_ITER, SUBLANES, LANES), F32),
        ],
        compiler_params=_cparams(("parallel", "parallel")),
        name=mode + "_prompt",
    )(*args)


def _diag_extract(row, n_rows, offset):
    r = lax.broadcasted_iota(jnp.int32, (n_rows, LANES), 0)
    l = lax.broadcasted_iota(jnp.int32, (n_rows, LANES), 1)
    return jnp.sum(jnp.where(l == r + offset, jnp.broadcast_to(row, (n_rows, LANES)), 0.0), axis=1, keepdims=True)


def _head_selector(n_heads):
    pad = -(-n_heads // BF16_SUBLANES) * BF16_SUBLANES
    r = lax.broadcasted_iota(jnp.int32, (pad, n_heads * LANES), 0)
    c = lax.broadcasted_iota(jnp.int32, (pad, n_heads * LANES), 1)
    one = (r == lax.shift_right_logical(c, int(math.log2(LANES)))).astype(BF16)
    return jnp.concatenate([one, one, one], axis=0)


def _head_columns(x, sel):
    pad = sel.shape[0] // 3
    if pad > x.shape[0]:
        x = jnp.concatenate([x, jnp.zeros((pad - x.shape[0], x.shape[1]), F32)], axis=0)
    return _dot_tn(jnp.concatenate(_split3(x), axis=0), sel)


def _dn_decode_kernel(p_ref, sm_ref, conv_ref, s_ref, cw_ref, alog_ref, dtb_ref, gain_ref,
                      o_ref, conv_out_ref, s_out_ref, *, n_heads, slab_g, bb):
    nh = n_heads
    sel = _head_selector(nh)
    cw = cw_ref[...]
    neg_a = -jnp.exp(alog_ref[...])
    dtb = dtb_ref[...]
    gain = gain_ref[...]
    for b in range(bb):
        x = p_ref[b, 0:3 * nh, :]
        y = cw[DN_CONV - 1] * x
        for i in range(DN_CONV - 1):
            y = y + cw[i] * conv_ref[b, i]
        for i in range(DN_CONV - 2):
            conv_out_ref[b, i] = conv_ref[b, i + 1]
        conv_out_ref[b, DN_CONV - 2] = x
        act = _silu(y)
        q = _l2norm(act[0:nh]) * (HEAD_DIM ** -0.5)
        k = _l2norm(act[nh:2 * nh])
        v = act[2 * nh:3 * nh]
        srow = sm_ref[b:b + 1, :]
        beta = _diag_extract(_sigmoid(srow), nh, 0)
        g = _diag_extract(neg_a * _softplus(srow + dtb), nh, nh)
        eg = jnp.exp(g)
        k_c = _head_columns(k, sel)
        q_c = _head_columns(q, sel)
        w_c = _head_columns(k * (beta * eg), sel)
        u = v * beta
        outs = []
        for h in range(nh):
            cols = slice(h * LANES, (h + 1) * LANES)
            s = s_ref[b, h]
            v_new = u[h:h + 1] - jnp.sum(w_c[:, cols] * s, axis=0, keepdims=True)
            s_new = s * eg[h:h + 1] + k_c[:, cols] * v_new
            s_out_ref[b, h] = s_new
            outs.append(jnp.sum(q_c[:, cols] * s_new, axis=0, keepdims=True))
        o = jnp.concatenate(outs, axis=0)
        gate = p_ref[b, slab_g:slab_g + nh, :]
        o_ref[b] = _gated_head_norm(o, gate, gain).astype(o_ref.dtype)


def _dn_decode(proj, sm, conv_state, s_state, cw, alog_vec, dtb_vec, gain, *, layer, n_heads, slab_g, bb):
    batch, n_slabs, _ = proj.shape
    conv_blk = (bb,) + conv_state.shape[2:]
    s_blk = (bb,) + s_state.shape[2:]
    vec_spec = pl.BlockSpec((1, LANES), lambda i: (0, 0))
    return pl.pallas_call(
        functools.partial(_dn_decode_kernel, n_heads=n_heads, slab_g=slab_g, bb=bb),
        out_shape=(jax.ShapeDtypeStruct((batch, n_heads, LANES), BF16),
                   jax.ShapeDtypeStruct(conv_state.shape[1:], F32),
                   jax.ShapeDtypeStruct(s_state.shape[1:], F32)),
        grid=(batch // bb,),
        in_specs=[
            pl.BlockSpec((bb, n_slabs, LANES), lambda i: (i, 0, 0)),
            pl.BlockSpec((bb, LANES), lambda i: (i, 0)),
            pl.BlockSpec((None,) + conv_blk, lambda i: (layer, i, 0, 0, 0)),
            pl.BlockSpec((None,) + s_blk, lambda i: (layer, i, 0, 0, 0)),
            pl.BlockSpec(cw.shape, lambda i: (0, 0, 0)),
            vec_spec, vec_spec, vec_spec,
        ],
        out_specs=(pl.BlockSpec((bb, n_heads, LANES), lambda i: (i, 0, 0)),
                   pl.BlockSpec(conv_blk, lambda i: (i, 0, 0, 0)),
                   pl.BlockSpec(s_blk, lambda i: (i, 0, 0, 0))),
        compiler_params=_cparams(("parallel",)),
        name="dn_decode",
    )(proj, sm, conv_state, s_state, cw, alog_vec, dtb_vec, gain)


def _gla_gate_decode_kernel(sm_ref, wg_ref, bias_ref, o_ref):
    z = _dot(sm_ref[...].astype(BF16), wg_ref[...]) + bias_ref[...]
    o_ref[...] = _log_sigmoid(z) * (1.0 / GLA_NORMALIZER)


def _gla_gate_decode(sm, wg_full, bias):
    m = sm.shape[0]
    n = wg_full.shape[1]
    return pl.pallas_call(
        _gla_gate_decode_kernel,
        out_shape=jax.ShapeDtypeStruct((m, n), F32),
        name="gla_gate_decode",
    )(sm, wg_full, bias)


def _gla_decode_kernel(*refs, mode, layer, n_heads, slabs, bb):
    if mode == "gla":
        p_ref, ga_ref, s_ref, gain_ref, o_ref, s_out_ref = refs
    else:
        p_ref, lbl_ref, s_ref, gain_ref, o_ref, s_out_ref = refs
    nh = n_heads
    sel = _head_selector(nh)
    sq, sk, sv, sg = slabs
    gain = gain_ref[...]
    if mode == "hg":
        lb = jnp.concatenate([_hg_lower_bound(lbl_ref[h], layer) for h in range(nh)], axis=0)
    for b in range(bb):
        if mode == "gla":
            q = p_ref[b, sq:sq + nh, :] * (HEAD_DIM ** -0.5)
            k = p_ref[b, sk:sk + nh, :]
            g = ga_ref[b]
        else:
            q = _silu(p_ref[b, sq:sq + nh, :])
            g, k = _hg_gates(p_ref[b, sk:sk + nh, :], lb)
        v = p_ref[b, sv:sv + nh, :]
        a_c = _head_columns(jnp.exp(g), sel)
        k_c = _head_columns(k, sel)
        q_c = _head_columns(q, sel)
        outs = []
        for h in range(nh):
            cols = slice(h * LANES, (h + 1) * LANES)
            s_new = a_c[:, cols] * s_ref[b, h] + k_c[:, cols] * v[h:h + 1]
            s_out_ref[b, h] = s_new
            outs.append(jnp.sum(q_c[:, cols] * s_new, axis=0, keepdims=True))
        o = jnp.concatenate(outs, axis=0)
        o_ref[b] = _gated_head_norm(o, p_ref[b, sg:sg + nh, :], gain).astype(o_ref.dtype)


def _gla_decode(proj, s_state, gain, *, mode, layer, n_heads, slabs, bb, ga=None, lbl=None):
    batch, n_slabs, _ = proj.shape
    s_blk = (bb,) + s_state.shape[2:]
    if mode == "gla":
        extra, extra_spec = ga, pl.BlockSpec((bb, n_heads, LANES), lambda i: (i, 0, 0))
    else:
        extra, extra_spec = lbl, pl.BlockSpec(lbl.shape, lambda i: (0, 0, 0))
    return pl.pallas_call(
        functools.partial(_gla_decode_kernel, mode=mode, layer=layer, n_heads=n_heads, slabs=slabs, bb=bb),
        out_shape=(jax.ShapeDtypeStruct((batch, n_heads, LANES), BF16),
                   jax.ShapeDtypeStruct(s_state.shape[1:], F32)),
        grid=(batch // bb,),
        in_specs=[
            pl.BlockSpec((bb, n_slabs, LANES), lambda i: (i, 0, 0)),
            extra_spec,
            pl.BlockSpec((None,) + s_blk, lambda i: (layer, i, 0, 0, 0)),
            pl.BlockSpec((1, LANES), lambda i: (0, 0)),
        ],
        out_specs=(pl.BlockSpec((bb, n_heads, LANES), lambda i: (i, 0, 0)),
                   pl.BlockSpec(s_blk, lambda i: (i, 0, 0, 0))),
        compiler_params=_cparams(("parallel",)),
        name=mode + "_decode",
    )(proj, extra, s_state, gain)


def _stacked_weights(w_out, ffn_w_gate, ffn_w_up, ffn_conv_b, ffn_w_down, dims):
    dn_dim, gla_dim = dims["dn_h"] * HEAD_DIM, dims["gla_h"] * HEAD_DIM
    cuts = (0, dn_dim, dn_dim + gla_dim, w_out.shape[1])
    return dict(
        wo=tuple(w_out[:, a:b, :].astype(BF16) for a, b in zip(cuts[:-1], cuts[1:])),
        ffn_wg=ffn_w_gate.astype(BF16), ffn_wu=ffn_w_up.astype(BF16), ffn_wd=ffn_w_down.astype(BF16),
        ffn_cb=ffn_conv_b.reshape(ffn_conv_b.shape[0], 1, -1),
    )


def _layer_weights(l, w_in, dn_conv_w, dn_a_log, dn_dt_bias, gla_w_gate, gla_gate_bias, dims):
    dn_h, gla_h, hg_h = dims["dn_h"], dims["gla_h"], dims["hg_h"]
    dn_dim, gla_dim, hg_dim = dn_h * HEAD_DIM, gla_h * HEAD_DIM, hg_h * HEAD_DIM
    sizes = (3 * dn_dim, dn_h, dn_h, dn_dim, gla_dim, gla_dim, gla_dim, GLA_LOWRANK, gla_dim,
             hg_dim, hg_dim, hg_dim, hg_dim)
    offs = [0]
    for s in sizes:
        offs.append(offs[-1] + s)
    wl = w_in[l]
    piece = lambda i: wl[:, offs[i]:offs[i + 1]]
    wide = [0, 3, 4, 5, 6, 8, 9, 10, 11, 12]
    w_big = jnp.concatenate([piece(i) for i in wide], axis=1).astype(BF16)
    small = jnp.concatenate([piece(1), piece(2), piece(7)], axis=1)
    n_small = small.shape[1]
    w_small = jnp.pad(small, ((0, 0), (0, LANES - n_small))).astype(BF16)

    lr0 = 2 * dn_h
    wg_rows = jnp.pad(gla_w_gate[l], ((lr0, LANES - lr0 - GLA_LOWRANK), (0, 0))).astype(BF16)
    return dict(
        w_big=w_big, w_small=w_small,
        dn_cw=dn_conv_w[l].reshape(DN_CONV, 3 * dn_h, HEAD_DIM),
        alog_vec=jnp.pad(dn_a_log[l], (dn_h, LANES - 2 * dn_h)).reshape(1, LANES),
        dtb_vec=jnp.pad(dn_dt_bias[l], (dn_h, LANES - 2 * dn_h)).reshape(1, LANES),
        gla_wg_full=wg_rows,
        gla_wg_heads=wg_rows.reshape(LANES, gla_h, HEAD_DIM).transpose(1, 0, 2),
        gla_bias=gla_gate_bias[l].reshape(1, gla_dim),
    )


def _slab_table(dims):
    dn_h, gla_h, hg_h = dims["dn_h"], dims["gla_h"], dims["hg_h"]
    names = [("dn_qkv", 3 * dn_h), ("dn_g", dn_h), ("gla_q", gla_h), ("gla_k", gla_h), ("gla_v", gla_h),
             ("gla_g", gla_h), ("hg_q", hg_h), ("hg_f", hg_h), ("hg_i", hg_h), ("hg_g", hg_h)]
    table, pos = {}, 0
    for n, cnt in names:
        table[n] = pos
        pos += cnt
    return table


def _prompt_layer(x, l, lw, sw, ffn_conv_w, norm_mix, dn_norm, gla_norm, hg_norm, hg_lbl, norm_ffn, dims, batch):
    m, d = x.shape
    seq = m // batch
    dn_h, gla_h, hg_h = dims["dn_h"], dims["gla_h"], dims["hg_h"]
    sl = _slab_table(dims)
    h = _rmsnorm(x, norm_mix[l], BF16)
    proj = _matmul([h], [lw["w_big"]], slab=True, tm=1024, tn=1024, name="in_proj")
    sm = _matmul([h], [lw["w_small"]], tm=1024, tn=LANES, name="in_proj_small")
    proj4 = proj.reshape(proj.shape[0], batch, seq, LANES)
    sm3 = sm.reshape(batch, seq, LANES)
    cw = lw["dn_cw"].transpose(1, 0, 2)
    o_dn, s_dn = _dn_prompt(proj4, sm3, cw, lw["alog_vec"], lw["dtb_vec"], dn_norm[l].reshape(1, LANES),
                            batch=batch, n_heads=dn_h, slab_q=sl["dn_qkv"], slab_g=sl["dn_g"])
    o_gla, s_gla = _gla_prompt(proj4, mode="gla", layer=l, batch=batch, n_heads=gla_h,
                               slabs=(sl["gla_q"], sl["gla_k"], sl["gla_v"], sl["gla_g"]),
                               sm=sm3, wg=lw["gla_wg_heads"], bias=lw["gla_bias"].reshape(gla_h, 1, LANES),
                               gain=gla_norm[l].reshape(1, LANES))
    o_hg, s_hg = _gla_prompt(proj4, mode="hg", layer=l, batch=batch, n_heads=hg_h,
                             slabs=(sl["hg_q"], sl["hg_f"], sl["hg_i"], sl["hg_g"]),
                             lbl=hg_lbl, gain=hg_norm[l].reshape(1, LANES))
    o_list = [o_dn.reshape(m, -1), o_gla.reshape(m, -1), o_hg.reshape(m, -1)]
    x = _matmul(o_list, list(sw["wo"]), res=x, tm=1024, tn=512, layer=l, name="out_proj")
    h2 = _rmsnorm(x, norm_ffn[l], BF16)
    hidden, tail = _ffn_up_prompt(h2, sw["ffn_wg"], sw["ffn_wu"], ffn_conv_w, sw["ffn_cb"],
                                  layer=l, batch=batch, tn=MXU_DIM)
    x = _matmul([hidden], [sw["ffn_wd"]], res=x, tm=512, tn=512, layer=l, name="ffn_down")
    n_qkv = 3 * dn_h
    conv_dn = proj4[sl["dn_qkv"]:sl["dn_qkv"] + n_qkv, :, seq - (DN_CONV - 1):, :]
    conv_dn = conv_dn.transpose(1, 2, 0, 3).reshape(batch, DN_CONV - 1, n_qkv * LANES)
    conv_ffn = tail[:, SUBLANES - (FFN_CONV - 1):, :]
    return x, (conv_dn, s_dn, s_gla, s_hg, conv_ffn)


def _decode_layer(x, l, lw, sw, ffn_conv_w, st_dn_conv, st_dn, st_gla, st_hg, st_ffn, norm_mix, dn_norm, gla_norm,
                  hg_norm, hg_lbl, norm_ffn, dims):
    m, d = x.shape
    depth = st_dn.shape[0]
    dn_h, gla_h, hg_h = dims["dn_h"], dims["gla_h"], dims["hg_h"]
    sl = _slab_table(dims)
    bb = SUBLANES
    h = _rmsnorm(x, norm_mix[l], BF16)
    proj = _matmul([h], [lw["w_big"]], tm=m, tn=1024, name="in_proj_dec")
    sm = _matmul([h], [lw["w_small"]], tm=m, tn=LANES, name="in_proj_small_dec")
    proj3 = proj.reshape(m, -1, LANES)
    conv_in = st_dn_conv.reshape(depth, m, DN_CONV - 1, 3 * dn_h, LANES)
    o_dn, conv_dn, s_dn = _dn_decode(proj3, sm, conv_in, st_dn, lw["dn_cw"], lw["alog_vec"], lw["dtb_vec"],
                                     dn_norm[l].reshape(1, LANES), layer=l, n_heads=dn_h, slab_g=sl["dn_g"], bb=bb)
    ga = _gla_gate_decode(sm, lw["gla_wg_full"], lw["gla_bias"]).reshape(m, gla_h, LANES)
    o_gla, s_gla = _gla_decode(proj3, st_gla, gla_norm[l].reshape(1, LANES), mode="gla", layer=l, n_heads=gla_h,
                               slabs=(sl["gla_q"], sl["gla_k"], sl["gla_v"], sl["gla_g"]), bb=bb, ga=ga)
    o_hg, s_hg = _gla_decode(proj3, st_hg, hg_norm[l].reshape(1, LANES), mode="hg", layer=l, n_heads=hg_h,
                             slabs=(sl["hg_q"], sl["hg_f"], sl["hg_i"], sl["hg_g"]), bb=bb, lbl=hg_lbl)
    o_list = [o_dn.reshape(m, -1), o_gla.reshape(m, -1), o_hg.reshape(m, -1)]
    x = _matmul(o_list, list(sw["wo"]), res=x, tm=m, tn=1024, layer=l, name="out_proj_dec")
    h2 = _rmsnorm(x, norm_ffn[l], BF16)
    st_flat = st_ffn.reshape(depth, m, -1)
    hidden, gate = _ffn_up_decode(h2, sw["ffn_wg"], sw["ffn_wu"], st_flat, ffn_conv_w, sw["ffn_cb"],
                                  layer=l, tn=MXU_DIM)
    x = _matmul([hidden], [sw["ffn_wd"]], res=x, tm=m, tn=512, layer=l, name="ffn_down_dec")
    conv_ffn = jnp.stack([st_ffn[l][:, 1, :], gate], axis=1)
    return x, (conv_dn.reshape(m, DN_CONV - 1, -1), s_dn, s_gla, s_hg, conv_ffn)


def kernel(x_prompt, x_sample, state_dn_conv, state_dn, state_gla, state_hg, state_ffn_conv, norm_mix, w_in, dn_conv_w, dn_a_log, dn_dt_bias, dn_norm, gla_w_gate, gla_gate_bias, gla_norm, hg_lb_logits, hg_norm, w_out, norm_ffn, ffn_w_gate, ffn_w_up, ffn_conv_w, ffn_conv_b, ffn_w_down, norm_final):
    batch, seq, d_model = x_prompt.shape
    dec_batch, dec_seq, _ = x_sample.shape
    assert dec_seq == 1 and FFN_CONV == 3
    depth = w_in.shape[0]
    dims = dict(dn_h=state_dn.shape[2], gla_h=state_gla.shape[2], hg_h=state_hg.shape[2])
    hg_lbl = hg_lb_logits.reshape(depth, dims["hg_h"], HEAD_DIM).transpose(1, 0, 2)

    xp = x_prompt.reshape(batch * seq, d_model)
    xs = x_sample.reshape(dec_batch, d_model)
    p_states, s_states = [], []
    sw = _stacked_weights(w_out, ffn_w_gate, ffn_w_up, ffn_conv_b, ffn_w_down, dims)
    for l in range(depth):
        lw = _layer_weights(l, w_in, dn_conv_w, dn_a_log, dn_dt_bias, gla_w_gate, gla_gate_bias, dims)
        xp, ps = _prompt_layer(xp, l, lw, sw, ffn_conv_w, norm_mix, dn_norm, gla_norm, hg_norm, hg_lbl, norm_ffn,
                               dims, batch)
        xs, ss = _decode_layer(xs, l, lw, sw, ffn_conv_w, state_dn_conv, state_dn, state_gla, state_hg,
                               state_ffn_conv, norm_mix, dn_norm, gla_norm, hg_norm, hg_lbl, norm_ffn, dims)
        p_states.append(ps)
        s_states.append(ss)
    y_prompt = _rmsnorm(xp, norm_final, F32).reshape(batch, seq, d_model)
    y_sample = _rmsnorm(xs, norm_final, F32).reshape(dec_batch, dec_seq, d_model)
    stack = lambda states, i: jnp.stack([s[i] for s in states], axis=0)
    return ((y_prompt, y_sample) + tuple(stack(p_states, i) for i in range(5))
            + tuple(stack(s_states, i) for i in range(5)))
```

```python
import functools
import math

import jax
import jax.numpy as jnp
from jax import lax
from jax.experimental import pallas as pl
from jax.experimental.pallas import tpu as pltpu

F32 = jnp.float32
BF16 = jnp.bfloat16

HEAD_DIM = 128
DN_CONV = 4
GLA_LOWRANK = 16
GLA_NORMALIZER = 16.0
FFN_CONV = 3
EPS = 1e-6
LB_FLOOR = 1e-30

LANES = 128
SUBLANES = 8
BF16_SUBLANES = 16
MXU_DIM = 256
VMEM_LIMIT_BYTES = 56 * 1024 * 1024

CHUNK = 64
SUB = 8


def _cparams(sem):
    return pltpu.CompilerParams(dimension_semantics=sem, vmem_limit_bytes=VMEM_LIMIT_BYTES)


def _sigmoid(x):
    return 1.0 / (1.0 + jnp.exp(-x))


def _silu(x):
    return x * _sigmoid(x)


def _softplus(x):
    return jnp.maximum(x, 0.0) + jnp.log1p(jnp.exp(-jnp.abs(x)))


def _log_sigmoid(x):
    return -_softplus(-x)


def _logaddexp(a, b):
    return jnp.maximum(a, b) + jnp.log1p(jnp.exp(-jnp.abs(a - b)))


def _dot(a, b):
    return jnp.dot(a, b, preferred_element_type=F32)


def _as_bf16(w):
    return w if w.dtype == BF16 else w.astype(BF16)


def _dot_nt(a, b, precision=None):
    return lax.dot_general(a, b, (((1,), (1,)), ((), ())), preferred_element_type=F32, precision=precision)


def _dot_tn(a, b):
    return lax.dot_general(a, b, (((0,), (0,)), ((), ())), preferred_element_type=F32)


def _tri_incl(n):
    r = lax.broadcasted_iota(jnp.int32, (n, n), 0)
    c = lax.broadcasted_iota(jnp.int32, (n, n), 1)
    return r, c


def _gated_head_norm(o, gate, gain):
    o = o * lax.rsqrt(jnp.mean(o * o, axis=-1, keepdims=True) + EPS) * gain
    return o * _silu(gate)


def _rmsnorm_kernel(x_ref, w_ref, o_ref):
    x = x_ref[...]
    y = x * lax.rsqrt(jnp.mean(x * x, axis=-1, keepdims=True) + EPS)
    o_ref[...] = (y * w_ref[...]).astype(o_ref.dtype)


def _rmsnorm(x, w, out_dtype):
    m, d = x.shape
    tm = min(m, 256)
    return pl.pallas_call(
        _rmsnorm_kernel,
        out_shape=jax.ShapeDtypeStruct((m, d), out_dtype),
        grid=(m // tm,),
        in_specs=[pl.BlockSpec((tm, d), lambda i: (i, 0)), pl.BlockSpec((1, d), lambda i: (0, 0))],
        out_specs=pl.BlockSpec((tm, d), lambda i: (i, 0)),
        compiler_params=_cparams(("parallel",)),
        name="rmsnorm",
    )(x, w.reshape(1, d))


def _mm_fullk_kernel(*refs, n_lhs, has_res, slab, w_transposed):
    lhs = refs[:n_lhs]
    w_ref = refs[n_lhs]
    res_ref = refs[n_lhs + 1] if has_res else None
    o_ref = refs[n_lhs + 1 + (1 if has_res else 0)]
    acc, row = None, 0
    for a in lhs:
        k = a.shape[1]
        if w_transposed:
            part = _dot_nt(a[...], _as_bf16(w_ref[:, row:row + k]))
        else:
            part = _dot(a[...], _as_bf16(w_ref[row:row + k, :]))
        acc = part if acc is None else acc + part
        row += k
    if has_res:
        acc = res_ref[...] + acc
    if slab:
        for j in range(o_ref.shape[0]):
            o_ref[j] = acc[:, j * LANES:(j + 1) * LANES]
    else:
        o_ref[...] = acc.astype(o_ref.dtype)


def _matmul(lhs_list, w, *, layer, res=None, slab=False, w_transposed=False, tm, tn, name):
    m = lhs_list[0].shape[0]
    kdim, n = (w.shape[2], w.shape[1]) if w_transposed else (w.shape[1], w.shape[2])
    tm = min(tm, m)
    tn = min(tn, n)
    assert sum(a.shape[1] for a in lhs_list) == kdim
    in_specs = [pl.BlockSpec((tm, a.shape[1]), lambda i, j: (i, 0)) for a in lhs_list]
    if w_transposed:
        in_specs.append(pl.BlockSpec((None, tn, kdim), lambda i, j: (layer, j, 0)))
    else:
        in_specs.append(pl.BlockSpec((None, kdim, tn), lambda i, j: (layer, 0, j)))
    args = list(lhs_list) + [w]
    if res is not None:
        in_specs.append(pl.BlockSpec((tm, tn), lambda i, j: (i, j)))
        args.append(res)
    if slab:
        out_shape = jax.ShapeDtypeStruct((n // LANES, m, LANES), F32)
        out_spec = pl.BlockSpec((tn // LANES, tm, LANES), lambda i, j: (j, i, 0))
    else:
        out_shape = jax.ShapeDtypeStruct((m, n), F32)
        out_spec = pl.BlockSpec((tm, tn), lambda i, j: (i, j))
    return pl.pallas_call(
        functools.partial(_mm_fullk_kernel, n_lhs=len(lhs_list), has_res=res is not None, slab=slab,
                          w_transposed=w_transposed),
        out_shape=out_shape,
        grid=(m // tm, n // tn),
        in_specs=in_specs,
        out_specs=out_spec,
        compiler_params=_cparams(("parallel", "arbitrary")),
        name=name,
    )(*args)


FFN_ROW_BLOCKS = 4


def _ffn_up_prompt_kernel(x_ref, wg_ref, wu_ref, cw_ref, cb_ref, hid_ref, tail_ref):
    wg = _as_bf16(wg_ref[...])
    wu = _as_bf16(wu_ref[...])
    cw = cw_ref[...]
    cb = cb_ref[...]
    rows = x_ref.shape[0] // FFN_ROW_BLOCKS

    def dots(r):
        x = x_ref[r * rows:(r + 1) * rows, :]
        return _dot(x, wg), _dot(x, wu)

    prev = jnp.zeros((SUBLANES, wg.shape[1]), F32)
    nxt = dots(0)
    for r in range(FFN_ROW_BLOCKS):
        gate, up = nxt
        if r + 1 < FFN_ROW_BLOCKS:
            nxt = dots(r + 1)
        cat = jnp.concatenate([prev, gate], axis=0)
        y = cw[FFN_CONV - 1:FFN_CONV] * gate
        for i in range(FFN_CONV - 1):
            back = FFN_CONV - 1 - i
            y = y + cw[i:i + 1] * cat[SUBLANES - back:SUBLANES - back + rows]
        hid_ref[r * rows:(r + 1) * rows, :] = (_silu(y + cb) * up).astype(hid_ref.dtype)
        prev = gate[rows - SUBLANES:]
    tail_ref[...] = prev


def _ffn_up_prompt(h, wg, wu, cw, cb, *, layer, batch, tn):
    m, d = h.shape
    n = wg.shape[2]
    seq = m // batch
    w_spec = pl.BlockSpec((None, d, tn), lambda b, j: (layer, 0, j))
    return pl.pallas_call(
        _ffn_up_prompt_kernel,
        out_shape=(jax.ShapeDtypeStruct((m, n), BF16), jax.ShapeDtypeStruct((batch, SUBLANES, n), F32)),
        grid=(batch, n // tn),
        in_specs=[
            pl.BlockSpec((seq, d), lambda b, j: (b, 0), pipeline_mode=pl.Buffered(1)),
            w_spec, w_spec,
            pl.BlockSpec((None, FFN_CONV, tn), lambda b, j: (layer, 0, j)),
            pl.BlockSpec((None, 1, tn), lambda b, j: (layer, 0, j)),
        ],
        out_specs=(
            pl.BlockSpec((seq, tn), lambda b, j: (b, j)),
            pl.BlockSpec((None, SUBLANES, tn), lambda b, j: (b, 0, j)),
        ),
        compiler_params=_cparams(("parallel", "parallel")),
        name="ffn_up_prompt",
    )(h, wg, wu, cw, cb)


def _ffn_up_decode_kernel(x_ref, wg_ref, wu_ref, s0_ref, s1_ref, cw_ref, cb_ref, hid_ref, gate_ref):
    x = x_ref[...]
    gate = _dot(x, _as_bf16(wg_ref[...]))
    up = _dot(x, _as_bf16(wu_ref[...]))
    cw = cw_ref[...]
    y = cw[0:1] * s0_ref[...] + cw[1:2] * s1_ref[...] + cw[2:3] * gate
    hid_ref[...] = (_silu(y + cb_ref[...]) * up).astype(hid_ref.dtype)
    gate_ref[...] = gate


def _ffn_up_decode(h, wg, wu, state, cw, cb, *, layer, tn):
    m, d = h.shape
    n = wg.shape[2]
    nt = n // tn
    w_spec = pl.BlockSpec((None, d, tn), lambda j: (layer, 0, j))
    return pl.pallas_call(
        _ffn_up_decode_kernel,
        out_shape=(jax.ShapeDtypeStruct((m, n), BF16), jax.ShapeDtypeStruct((m, n), F32)),
        grid=(nt,),
        in_specs=[
            pl.BlockSpec((m, d), lambda j: (0, 0)),
            w_spec, w_spec,
            pl.BlockSpec((None, m, tn), lambda j: (layer, 0, j)),
            pl.BlockSpec((None, m, tn), lambda j: (layer, 0, nt + j)),
            pl.BlockSpec((None, FFN_CONV, tn), lambda j: (layer, 0, j)),
            pl.BlockSpec((None, 1, tn), lambda j: (layer, 0, j)),
        ],
        out_specs=(pl.BlockSpec((m, tn), lambda j: (0, j)), pl.BlockSpec((m, tn), lambda j: (0, j))),
        compiler_params=_cparams(("parallel",)),
        name="ffn_up_decode",
    )(h, wg, wu, state, state, cw, cb)


def _lane_select(x, lane_index):
    lane = lax.broadcasted_iota(jnp.int32, x.shape, 1)
    return jnp.sum(jnp.where(lane == lane_index, x, 0.0), axis=1, keepdims=True)


def _causal_conv_chunk(x_ref, cw, c, rows):
    start = pl.multiple_of(c * rows, rows)
    xc = x_ref[pl.ds(start, rows), :]
    pstart = pl.multiple_of(jnp.maximum(c * rows - SUBLANES, 0), SUBLANES)
    xp = x_ref[pl.ds(pstart, SUBLANES), :]
    xp = jnp.where(c == 0, 0.0, xp)
    cat = jnp.concatenate([xp, xc], axis=0)
    y = cw[DN_CONV - 1:DN_CONV] * xc
    for i in range(DN_CONV - 1):
        back = DN_CONV - 1 - i
        y = y + cw[i:i + 1] * cat[SUBLANES - back:SUBLANES - back + rows]
    return y


def _l2norm(x):
    return x * lax.rsqrt(jnp.sum(x * x, axis=-1, keepdims=True) + EPS)


def _split2(x):
    hi = x.astype(BF16)
    lo = (x - hi.astype(F32)).astype(BF16)
    return hi, lo


def _split3(x):
    hi = x.astype(BF16)
    r1 = x - hi.astype(F32)
    mid = r1.astype(BF16)
    lo = (r1 - mid.astype(F32)).astype(BF16)
    return hi, mid, lo


def _dot3(a, b, dot=_dot):
    a_hi, a_lo = _split2(a)
    b_hi, b_lo = _split2(b)
    return dot(a_hi, b_hi) + (dot(a_hi, b_lo) + dot(a_lo, b_hi))


def _dot2(a, b):
    b_hi, b_lo = _split2(b)
    a_b = a.astype(BF16)
    return _dot(a_b, b_hi) + _dot(a_b, b_lo)


def _cumsum_rows(tri_bf16, g):
    hi, mid, lo = _split3(g)
    return _dot(tri_bf16, hi) + (_dot(tri_bf16, mid) + _dot(tri_bf16, lo))


def _unit_lower_solve(mms, rhss):
    n = mms[0].shape[0]
    xs = [-mm for mm in mms]
    rs = [rhs + _dot2(x, rhs) for x, rhs in zip(xs, rhss)]
    for _ in range(int(math.log2(n)) - 1):
        xs = [_dot2(x, x) for x in xs]
        rs = [r + _dot2(x, r) for x, r in zip(xs, rs)]
    return rs


DN_HEADS_PER_STEP = 2
DN_CHUNKS_PER_ITER = 4


def _dn_prompt_kernel(q_ref, k_ref, v_ref, gt_ref, sm_ref, cwq_ref, cwk_ref, cwv_ref, alog_ref, dtb_ref, gain_ref,
                      o_ref, s_ref, u_s, w_s, qe_s, kd_s, attn_s, gend_s, st_s, *, n_heads):
    hp_n = q_ref.shape[0]
    head0 = pl.program_id(1) * hp_n
    seq = q_ref.shape[1]
    n_chunks = seq // CHUNK
    r, c = _tri_incl(CHUNK)
    tri = (r >= c).astype(BF16)
    incl = r >= c
    strict = r > c
    neg_a = -jnp.exp(alog_ref[...])
    dtb = dtb_ref[...]

    def phase1(it, carry):
        items = [(hp, it * DN_CHUNKS_PER_ITER + u) for hp in range(hp_n) for u in range(DN_CHUNKS_PER_ITER)]
        pre = []
        for hp, ci in items:
            rows = pl.ds(pl.multiple_of(ci * CHUNK, CHUNK), CHUNK)
            q = _silu(_causal_conv_chunk(q_ref.at[hp], cwq_ref[hp], ci, CHUNK))
            k = _silu(_causal_conv_chunk(k_ref.at[hp], cwk_ref[hp], ci, CHUNK))
            v = _silu(_causal_conv_chunk(v_ref.at[hp], cwv_ref[hp], ci, CHUNK))
            q = _l2norm(q) * (HEAD_DIM ** -0.5)
            k = _l2norm(k)
            sm = sm_ref[rows, :]
            beta = _lane_select(_sigmoid(sm), head0 + hp)
            g = _lane_select(neg_a * _softplus(sm + dtb), n_heads + head0 + hp)
            pre.append((q, k, v, beta, jnp.broadcast_to(g, (CHUNK, LANES))))
        big_gs = [_cumsum_rows(tri, p[4]) for p in pre]
        mms, rhss, gammas, egs = [], [], [], []
        for (q, k, v, beta, _), big_g in zip(pre, big_gs):
            g_row = big_g.T[:CHUNK, :]
            g_col = big_g[:, :CHUNK]
            gamma = jnp.where(incl, jnp.exp(jnp.where(incl, g_col - g_row, 0.0)), 0.0)
            kb = k.astype(BF16)
            mms.append(jnp.where(strict, beta * _dot_nt(kb, kb) * gamma, 0.0))
            eg = jnp.exp(big_g)
            rhss.append(jnp.concatenate([v * beta, k * (beta * eg)], axis=1))
            gammas.append(gamma)
            egs.append(eg)
        uws = _unit_lower_solve(mms, rhss)
        for (hp, ci), (q, k, v, beta, _), big_g, gamma, eg, uw in zip(items, pre, big_gs, gammas, egs, uws):
            rows = pl.ds(pl.multiple_of(ci * CHUNK, CHUNK), CHUNK)
            g_end = big_g[CHUNK - 1:CHUNK, :]
            u_s[hp, rows, :] = uw[:, :HEAD_DIM]
            w_s[hp, rows, :] = uw[:, HEAD_DIM:]
            qe_s[hp, rows, :] = q * eg
            kd_s[hp, rows, :] = k * jnp.exp(g_end - big_g)
            attn_s[hp, ci] = _dot_nt(q.astype(BF16), k.astype(BF16)) * gamma
            gend_s[hp, ci] = jnp.broadcast_to(jnp.exp(g_end), (SUBLANES, LANES))
        return carry

    lax.fori_loop(0, n_chunks // DN_CHUNKS_PER_ITER, phase1, 0)

    st_s[...] = jnp.zeros_like(st_s)
    gain = gain_ref[...]

    def phase2(ci, carry):
        rows = pl.ds(pl.multiple_of(ci * CHUNK, CHUNK), CHUNK)
        ss = [st_s[hp] for hp in range(hp_n)]
        sbs = [s.astype(BF16) for s in ss]
        vbs = [(u_s[hp, rows, :] - _dot(w_s[hp, rows, :].astype(BF16), sbs[hp])).astype(BF16) for hp in range(hp_n)]
        for hp in range(hp_n):
            o = _dot(qe_s[hp, rows, :].astype(BF16), sbs[hp]) + _dot(attn_s[hp, ci].astype(BF16), vbs[hp])
            st_s[hp] = ss[hp] * gend_s[hp, ci][0:1, :] + _dot_tn(kd_s[hp, rows, :].astype(BF16), vbs[hp])
            o_ref[rows, hp * LANES:(hp + 1) * LANES] = _gated_head_norm(o, gt_ref[hp, rows, :], gain).astype(o_ref.dtype)
        return carry

    lax.fori_loop(0, n_chunks, phase2, 0)
    s_ref[...] = st_s[...]


def _dn_prompt(proj, sm, cw, alog_vec, dtb_vec, gain, *, batch, n_heads, slab_q, slab_g):
    seq = proj.shape[2]
    n_chunks = seq // CHUNK
    hp = math.gcd(DN_HEADS_PER_STEP, n_heads)
    assert slab_q % hp == 0 and slab_g % hp == 0 and n_chunks % DN_CHUNKS_PER_ITER == 0

    def slab_spec(first):
        return pl.BlockSpec((hp, None, seq, LANES), lambda b, j: (first // hp + j, b, 0, 0))

    def cw_spec(first):
        return pl.BlockSpec((hp, DN_CONV, LANES), lambda b, j: (first // hp + j, 0, 0))

    vec_spec = pl.BlockSpec((1, LANES), lambda b, j: (0, 0))
    return pl.pallas_call(
        functools.partial(_dn_prompt_kernel, n_heads=n_heads),
        out_shape=(jax.ShapeDtypeStruct((batch, seq, n_heads * LANES), BF16),
                   jax.ShapeDtypeStruct((batch, n_heads, HEAD_DIM, HEAD_DIM), F32)),
        grid=(batch, n_heads // hp),
        in_specs=[
            slab_spec(slab_q), slab_spec(slab_q + n_heads), slab_spec(slab_q + 2 * n_heads), slab_spec(slab_g),
            pl.BlockSpec((None, seq, LANES), lambda b, j: (b, 0, 0)),
            cw_spec(0), cw_spec(n_heads), cw_spec(2 * n_heads),
            vec_spec, vec_spec, vec_spec,
        ],
        out_specs=(pl.BlockSpec((None, seq, hp * LANES), lambda b, j: (b, 0, j)),
                   pl.BlockSpec((None, hp, HEAD_DIM, HEAD_DIM), lambda b, j: (b, j, 0, 0))),
        scratch_shapes=[pltpu.VMEM((hp, seq, LANES), F32)] * 4 + [
            pltpu.VMEM((hp, n_chunks, CHUNK, CHUNK), F32),
            pltpu.VMEM((hp, n_chunks, SUBLANES, LANES), F32),
            pltpu.VMEM((hp, HEAD_DIM, HEAD_DIM), F32),
        ],
        compiler_params=_cparams(("parallel", "parallel")),
        name="dn_prompt",
    )(proj, proj, proj, proj, sm, cw, cw, cw, alog_vec, dtb_vec, gain)


def _hg_lower_bound(logits, layer):
    mx = jnp.max(logits, axis=0, keepdims=True)
    e = jnp.exp(logits - mx)
    p = e / jnp.sum(e, axis=0, keepdims=True)
    lb = jnp.zeros((1, logits.shape[1]), F32)
    for i in range(1, layer + 1):
        lb = lb + p[i:i + 1]
    return lb


def _hg_gates(zf, lb):
    e = jnp.exp(-jnp.abs(zf))
    log_sig = jnp.minimum(zf, 0.0) - jnp.log1p(e)
    log_f = _logaddexp(jnp.log(jnp.maximum(lb, LB_FLOOR)), jnp.log1p(-lb) + log_sig)
    k = (1.0 - lb) * (jnp.where(zf >= 0.0, e, 1.0) / (1.0 + e))
    return log_f, k


def _gla_intra_off(q, k, big_g):
    off = []
    for lo in range(SUB, q.shape[0], SUB):
        ref = big_g[lo - 1:lo]
        qt = q[lo:lo + SUB] * jnp.exp(big_g[lo:lo + SUB] - ref)
        kt = k[:lo] * jnp.exp(ref - big_g[:lo])
        off.append(_dot3(qt, kt, _dot_nt))
    return off


def _gla_intra_diag(q, k, big_g):
    n = q.shape[0]
    lane = lax.broadcasted_iota(jnp.int32, (SUB, n), 1)
    diag = []
    for lo in range(0, n, SUB):
        qi = q[lo:lo + SUB]
        gi = big_g[lo:lo + SUB]
        ki = k[lo:lo + SUB]
        blk = jnp.zeros((SUB, n), F32)
        for j in range(SUB):
            dec = jnp.exp(jnp.minimum(gi - gi[j:j + 1], 0.0))
            col = jnp.sum(qi * ki[j:j + 1] * dec, axis=1, keepdims=True)
            blk = jnp.where(lane == lo + j, col, blk)
        diag.append(blk)
    return diag


def _gla_intra_merge(off, diag):
    n = diag[0].shape[1]
    rows = [diag[0]]
    for bi, o in enumerate(off, start=1):
        rows.append(diag[bi] + jnp.concatenate([o, jnp.zeros((SUB, n - bi * SUB), F32)], axis=1))
    r, c = _tri_incl(n)
    return jnp.where(r >= c, jnp.concatenate(rows, axis=0), 0.0)


GLA_CHUNKS_PER_ITER = 8


def _gla_prompt_kernel(*refs, mode, layer):
    if mode == "gla":
        (q_ref, k_ref, v_ref, gt_ref, sm_ref, wg_ref, bias_ref, gain_ref, o_ref, s_ref,
         st_s, a_s, qe_s, kd_s, de_s) = refs
    else:
        (q_ref, k_ref, v_ref, gt_ref, lbl_ref, gain_ref, o_ref, s_ref, st_s, a_s, qe_s, kd_s, de_s) = refs
        lb = _hg_lower_bound(lbl_ref[...], layer)
    seq = q_ref.shape[0]
    n_groups = seq // (CHUNK * GLA_CHUNKS_PER_ITER)
    r, c = _tri_incl(CHUNK)
    tri = (r >= c).astype(BF16)
    gain = gain_ref[...]
    st_s[...] = jnp.zeros_like(st_s)

    def group_rows(it):
        return [pl.ds(pl.multiple_of((it * GLA_CHUNKS_PER_ITER + u) * CHUNK, CHUNK), CHUNK)
                for u in range(GLA_CHUNKS_PER_ITER)]

    def front(it):
        all_rows, pre = group_rows(it), []
        for rows in all_rows:
            if mode == "gla":
                q = q_ref[rows, :] * (HEAD_DIM ** -0.5)
                k = k_ref[rows, :]
                z = _dot(sm_ref[rows, :].astype(BF16), wg_ref[...]) + bias_ref[...]
                g = _log_sigmoid(z) * (1.0 / GLA_NORMALIZER)
            else:
                q = _silu(q_ref[rows, :])
                g, k = _hg_gates(k_ref[rows, :], lb)
            pre.append((q, k, g))
        big_gs = [_cumsum_rows(tri, g) for _, _, g in pre]
        offs = [_gla_intra_off(q, k, big_g) for (q, k, _), big_g in zip(pre, big_gs)]
        diags = [_gla_intra_diag(q, k, big_g) for (q, k, _), big_g in zip(pre, big_gs)]
        for u, ((q, k, _), big_g, off, diag) in enumerate(zip(pre, big_gs, offs, diags)):
            g_end = big_g[CHUNK - 1:CHUNK, :]
            a_s[u] = _gla_intra_merge(off, diag).astype(BF16)
            qe_s[u] = (q * jnp.exp(big_g)).astype(BF16)
            kd_s[u] = (k * jnp.exp(g_end - big_g)).astype(BF16)
            de_s[u] = jnp.broadcast_to(jnp.exp(g_end), (SUBLANES, LANES))

    def back(it):
        all_rows = group_rows(it)
        vbs = [v_ref[rows, :].astype(BF16) for rows in all_rows]
        intra = [_dot(a_s[u], vb) for u, vb in enumerate(vbs)]
        incr = [_dot_tn(vb, kd_s[u]) for u, vb in enumerate(vbs)]
        states = [st_s[...]]
        for u, t in enumerate(incr):
            states.append(states[-1] * de_s[u][0:1, :] + t)
        for u, (rows, st, av) in enumerate(zip(all_rows, states, intra)):
            o = _dot_nt(qe_s[u], st.astype(BF16)) + av
            o_ref[rows, :] = _gated_head_norm(o, gt_ref[rows, :], gain).astype(o_ref.dtype)
        st_s[...] = states[-1]

    front(0)

    def body(it, carry):
        back(it - 1)
        front(it)
        return carry

    lax.fori_loop(1, n_groups, body, 0)
    back(n_groups - 1)
    s_ref[...] = st_s[...].T


def _gla_prompt(proj, *, mode, layer, batch, n_heads, slabs, sm=None, wg=None, bias=None, lbl=None, gain=None):
    seq = proj.shape[2]

    def slab_spec(first):
        return pl.BlockSpec((None, None, seq, LANES), lambda b, h: (first + h, b, 0, 0))

    in_specs = [slab_spec(s) for s in slabs]
    args = [proj] * 4
    if mode == "gla":
        in_specs += [
            pl.BlockSpec((None, seq, LANES), lambda b, h: (b, 0, 0)),
            pl.BlockSpec((None, LANES, LANES), lambda b, h: (h, 0, 0)),
            pl.BlockSpec((None, 1, LANES), lambda b, h: (h, 0, 0)),
        ]
        args += [sm, wg, bias]
    else:
        in_specs += [pl.BlockSpec((None, lbl.shape[1], LANES), lambda b, h: (h, 0, 0))]
        args += [lbl]
    in_specs.append(pl.BlockSpec((1, LANES), lambda b, h: (0, 0)))
    args.append(gain)
    return pl.pallas_call(
        functools.partial(_gla_prompt_kernel, mode=mode, layer=layer),
        out_shape=(jax.ShapeDtypeStruct((batch, seq, n_heads * LANES), BF16),
                   jax.ShapeDtypeStruct((batch, n_heads, HEAD_DIM, HEAD_DIM), F32)),
        grid=(batch, n_heads),
        in_specs=in_specs,
        out_specs=(pl.BlockSpec((None, seq, LANES), lambda b, h: (b, 0, h)),
                   pl.BlockSpec((None, None, HEAD_DIM, HEAD_DIM), lambda b, h: (b, h, 0, 0))),
        scratch_shapes=[
            pltpu.VMEM((HEAD_DIM, HEAD_DIM), F32),
            pltpu.VMEM((GLA_CHUNKS_PER_ITER, CHUNK, CHUNK), BF16),
            pltpu.VMEM((GLA_CHUNKS_PER_ITER, CHUNK, HEAD_DIM), BF16),
            pltpu.VMEM((GLA_CHUNKS_PER_ITER, CHUNK, HEAD_DIM), BF16),
            pltpu.VMEM((GLA_CHUNKS_PER_ITER, SUBLANES, LANES), F32),
        ],
        compiler_params=_cparams(("parallel", "parallel")),
        name=mode + "_prompt",
    )(*args)


def _diag_extract(row, n_rows, offset):
    r = lax.broadcasted_iota(jnp.int32, (n_rows, LANES), 0)
    l = lax.broadcasted_iota(jnp.int32, (n_rows, LANES), 1)
    return jnp.sum(jnp.where(l == r + offset, jnp.broadcast_to(row, (n_rows, LANES)), 0.0), axis=1, keepdims=True)


def _head_selector(n_heads):
    pad = -(-n_heads // BF16_SUBLANES) * BF16_SUBLANES
    r = lax.broadcasted_iota(jnp.int32, (pad, n_heads * LANES), 0)
    c = lax.broadcasted_iota(jnp.int32, (pad, n_heads * LANES), 1)
    one = (r == lax.shift_right_logical(c, int(math.log2(LANES)))).astype(BF16)
    return jnp.concatenate([one, one, one], axis=0)


def _head_columns(x, sel):
    pad = sel.shape[0] // 3
    if pad > x.shape[0]:
        x = jnp.concatenate([x, jnp.zeros((pad - x.shape[0], x.shape[1]), F32)], axis=0)
    return _dot_tn(jnp.concatenate(_split3(x), axis=0), sel)


def _dn_decode_kernel(p_ref, sm_ref, conv_ref, s_ref, cw_ref, alog_ref, dtb_ref, gain_ref,
                      o_ref, conv_out_ref, s_out_ref, *, n_heads, slab_g, bb):
    nh = n_heads
    sel = _head_selector(nh)
    cw = cw_ref[...]
    neg_a = -jnp.exp(alog_ref[...])
    dtb = dtb_ref[...]
    gain = gain_ref[...]
    for b in range(bb):
        x = p_ref[b, 0:3 * nh, :]
        y = cw[DN_CONV - 1] * x
        for i in range(DN_CONV - 1):
            y = y + cw[i] * conv_ref[b, i]
        for i in range(DN_CONV - 2):
            conv_out_ref[b, i] = conv_ref[b, i + 1]
        conv_out_ref[b, DN_CONV - 2] = x
        act = _silu(y)
        q = _l2norm(act[0:nh]) * (HEAD_DIM ** -0.5)
        k = _l2norm(act[nh:2 * nh])
        v = act[2 * nh:3 * nh]
        srow = sm_ref[b:b + 1, :]
        beta = _diag_extract(_sigmoid(srow), nh, 0)
        g = _diag_extract(neg_a * _softplus(srow + dtb), nh, nh)
        eg = jnp.exp(g)
        k_c = _head_columns(k, sel)
        q_c = _head_columns(q, sel)
        w_c = _head_columns(k * (beta * eg), sel)
        u = v * beta
        outs = []
        for h in range(nh):
            cols = slice(h * LANES, (h + 1) * LANES)
            s = s_ref[b, h]
            v_new = u[h:h + 1] - jnp.sum(w_c[:, cols] * s, axis=0, keepdims=True)
            s_new = s * eg[h:h + 1] + k_c[:, cols] * v_new
            s_out_ref[b, h] = s_new
            outs.append(jnp.sum(q_c[:, cols] * s_new, axis=0, keepdims=True))
        o = jnp.concatenate(outs, axis=0)
        gate = p_ref[b, slab_g:slab_g + nh, :]
        o_ref[b] = _gated_head_norm(o, gate, gain).astype(o_ref.dtype)


def _dn_decode(proj, sm, conv_state, s_state, cw, alog_vec, dtb_vec, gain, *, layer, n_heads, slab_g, bb):
    batch, n_slabs, _ = proj.shape
    conv_blk = (bb,) + conv_state.shape[2:]
    s_blk = (bb,) + s_state.shape[2:]
    vec_spec = pl.BlockSpec((1, LANES), lambda i: (0, 0))
    return pl.pallas_call(
        functools.partial(_dn_decode_kernel, n_heads=n_heads, slab_g=slab_g, bb=bb),
        out_shape=(jax.ShapeDtypeStruct((batch, n_heads, LANES), BF16),
                   jax.ShapeDtypeStruct(conv_state.shape[1:], F32),
                   jax.ShapeDtypeStruct(s_state.shape[1:], F32)),
        grid=(batch // bb,),
        in_specs=[
            pl.BlockSpec((bb, n_slabs, LANES), lambda i: (i, 0, 0)),
            pl.BlockSpec((bb, LANES), lambda i: (i, 0)),
            pl.BlockSpec((None,) + conv_blk, lambda i: (layer, i, 0, 0, 0)),
            pl.BlockSpec((None,) + s_blk, lambda i: (layer, i, 0, 0, 0)),
            pl.BlockSpec(cw.shape, lambda i: (0, 0, 0)),
            vec_spec, vec_spec, vec_spec,
        ],
        out_specs=(pl.BlockSpec((bb, n_heads, LANES), lambda i: (i, 0, 0)),
                   pl.BlockSpec(conv_blk, lambda i: (i, 0, 0, 0)),
                   pl.BlockSpec(s_blk, lambda i: (i, 0, 0, 0))),
        compiler_params=_cparams(("parallel",)),
        name="dn_decode",
    )(proj, sm, conv_state, s_state, cw, alog_vec, dtb_vec, gain)


def _gla_gate_decode_kernel(sm_ref, wg_ref, bias_ref, o_ref):
    z = _dot(sm_ref[...].astype(BF16), wg_ref[...]) + bias_ref[...]
    o_ref[...] = _log_sigmoid(z) * (1.0 / GLA_NORMALIZER)


def _gla_gate_decode(sm, wg_full, bias):
    m = sm.shape[0]
    n = wg_full.shape[1]
    return pl.pallas_call(
        _gla_gate_decode_kernel,
        out_shape=jax.ShapeDtypeStruct((m, n), F32),
        name="gla_gate_decode",
    )(sm, wg_full, bias)


def _gla_decode_kernel(*refs, mode, layer, n_heads, slabs, bb):
    if mode == "gla":
        p_ref, ga_ref, s_ref, gain_ref, o_ref, s_out_ref = refs
    else:
        p_ref, lbl_ref, s_ref, gain_ref, o_ref, s_out_ref = refs
    nh = n_heads
    sel = _head_selector(nh)
    sq, sk, sv, sg = slabs
    gain = gain_ref[...]
    if mode == "hg":
        lb = jnp.concatenate([_hg_lower_bound(lbl_ref[h], layer) for h in range(nh)], axis=0)
    for b in range(bb):
        if mode == "gla":
            q = p_ref[b, sq:sq + nh, :] * (HEAD_DIM ** -0.5)
            k = p_ref[b, sk:sk + nh, :]
            g = ga_ref[b]
        else:
            q = _silu(p_ref[b, sq:sq + nh, :])
            g, k = _hg_gates(p_ref[b, sk:sk + nh, :], lb)
        v = p_ref[b, sv:sv + nh, :]
        a_c = _head_columns(jnp.exp(g), sel)
        k_c = _head_columns(k, sel)
        q_c = _head_columns(q, sel)
        outs = []
        for h in range(nh):
            cols = slice(h * LANES, (h + 1) * LANES)
            s_new = a_c[:, cols] * s_ref[b, h] + k_c[:, cols] * v[h:h + 1]
            s_out_ref[b, h] = s_new
            outs.append(jnp.sum(q_c[:, cols] * s_new, axis=0, keepdims=True))
        o = jnp.concatenate(outs, axis=0)
        o_ref[b] = _gated_head_norm(o, p_ref[b, sg:sg + nh, :], gain).astype(o_ref.dtype)


def _gla_decode(proj, s_state, gain, *, mode, layer, n_heads, slabs, bb, ga=None, lbl=None):
    batch, n_slabs, _ = proj.shape
    s_blk = (bb,) + s_state.shape[2:]
    if mode == "gla":
        extra, extra_spec = ga, pl.BlockSpec((bb, n_heads, LANES), lambda i: (i, 0, 0))
    else:
        extra, extra_spec = lbl, pl.BlockSpec(lbl.shape, lambda i: (0, 0, 0))
    return pl.pallas_call(
        functools.partial(_gla_decode_kernel, mode=mode, layer=layer, n_heads=n_heads, slabs=slabs, bb=bb),
        out_shape=(jax.ShapeDtypeStruct((batch, n_heads, LANES), BF16),
                   jax.ShapeDtypeStruct(s_state.shape[1:], F32)),
        grid=(batch // bb,),
        in_specs=[
            pl.BlockSpec((bb, n_slabs, LANES), lambda i: (i, 0, 0)),
            extra_spec,
            pl.BlockSpec((None,) + s_blk, lambda i: (layer, i, 0, 0, 0)),
            pl.BlockSpec((1, LANES), lambda i: (0, 0)),
        ],
        out_specs=(pl.BlockSpec((bb, n_heads, LANES), lambda i: (i, 0, 0)),
                   pl.BlockSpec(s_blk, lambda i: (i, 0, 0, 0))),
        compiler_params=_cparams(("parallel",)),
        name=mode + "_decode",
    )(proj, extra, s_state, gain)


def _in_proj_layout(w_in, dims):
    dn_h, gla_h, hg_h = dims["dn_h"], dims["gla_h"], dims["hg_h"]
    dn_dim, gla_dim, hg_dim = dn_h * HEAD_DIM, gla_h * HEAD_DIM, hg_h * HEAD_DIM
    sizes = (3 * dn_dim, dn_h, dn_h, dn_dim, gla_dim, gla_dim, gla_dim, GLA_LOWRANK, gla_dim,
             hg_dim, hg_dim, hg_dim, hg_dim)
    offs = [0]
    for s in sizes:
        offs.append(offs[-1] + s)
    wide = ((offs[0], offs[1]), (offs[3], offs[7]), (offs[8], offs[13]))
    narrow = ((offs[1], offs[3]), (offs[7], offs[8]))
    depth, d, d_in = w_in.shape
    wt = jnp.swapaxes(w_in, 1, 2)
    used = sum(b - a for a, b in narrow)
    small = jnp.concatenate([wt[:, a:b] for a, b in narrow], axis=1)
    small = jnp.pad(small, ((0, 0), (0, LANES - used), (0, 0)))
    return wt.reshape(depth * d_in, d), wide, small


def _in_proj_kernel(x_ref, w_ref, o_ref, *, slab):
    acc = _dot_nt(x_ref[...], _as_bf16(w_ref[...]))
    if slab:
        for j in range(o_ref.shape[0]):
            o_ref[j] = acc[:, j * LANES:(j + 1) * LANES]
    else:
        o_ref[...] = acc


def _in_proj(h, wt, wide, *, layer, d_in, slab, tm, tn, name):
    m, d = h.shape
    n = sum(b - a for a, b in wide)
    tm = min(tm, m)
    starts, pos = [], 0
    for a, b in wide:
        assert (b - a) % tn == 0 and a % SUBLANES == 0
        starts.append((pos, a - pos))
        pos += b - a

    def w_index(i, j):
        col = j * tn
        shift = starts[0][1]
        for first, sh in starts[1:]:
            shift = jnp.where(col >= first, sh, shift)
        return (pl.multiple_of(layer * d_in + col + shift, SUBLANES), 0)

    if slab:
        out_shape = jax.ShapeDtypeStruct((n // LANES, m, LANES), F32)
        out_spec = pl.BlockSpec((tn // LANES, tm, LANES), lambda i, j: (j, i, 0))
    else:
        out_shape = jax.ShapeDtypeStruct((m, n), F32)
        out_spec = pl.BlockSpec((tm, tn), lambda i, j: (i, j))
    return pl.pallas_call(
        functools.partial(_in_proj_kernel, slab=slab),
        out_shape=out_shape,
        grid=(m // tm, n // tn),
        in_specs=[pl.BlockSpec((tm, d), lambda i, j: (i, 0), pipeline_mode=pl.Buffered(1)),
                  pl.BlockSpec((pl.Element(tn), pl.Element(d)), w_index)],
        out_specs=out_spec,
        compiler_params=_cparams(("parallel", "arbitrary")),
        name=name,
    )(h, wt)


def _layer_weights(l, dn_conv_w, dn_a_log, dn_dt_bias, gla_w_gate, gla_gate_bias, dims):
    dn_h, gla_h = dims["dn_h"], dims["gla_h"]
    gla_dim = gla_h * HEAD_DIM
    lr0 = 2 * dn_h
    wg_rows = jnp.pad(gla_w_gate[l], ((lr0, LANES - lr0 - GLA_LOWRANK), (0, 0))).astype(BF16)
    return dict(
        dn_cw=dn_conv_w[l].reshape(DN_CONV, 3 * dn_h, HEAD_DIM),
        alog_vec=jnp.pad(dn_a_log[l], (dn_h, LANES - 2 * dn_h)).reshape(1, LANES),
        dtb_vec=jnp.pad(dn_dt_bias[l], (dn_h, LANES - 2 * dn_h)).reshape(1, LANES),
        gla_wg_full=wg_rows,
        gla_wg_heads=wg_rows.reshape(LANES, gla_h, HEAD_DIM).transpose(1, 0, 2),
        gla_bias=gla_gate_bias[l].reshape(1, gla_dim),
    )


def _slab_table(dims):
    dn_h, gla_h, hg_h = dims["dn_h"], dims["gla_h"], dims["hg_h"]
    names = [("dn_qkv", 3 * dn_h), ("dn_g", dn_h), ("gla_q", gla_h), ("gla_k", gla_h), ("gla_v", gla_h),
             ("gla_g", gla_h), ("hg_q", hg_h), ("hg_f", hg_h), ("hg_i", hg_h), ("hg_g", hg_h)]
    table, pos = {}, 0
    for n, cnt in names:
        table[n] = pos
        pos += cnt
    return table


def _prompt_layer(x, l, lw, sw, ffn_conv_w, norm_mix, dn_norm, gla_norm, hg_norm, hg_lbl, norm_ffn, dims, batch):
    m, d = x.shape
    seq = m // batch
    dn_h, gla_h, hg_h = dims["dn_h"], dims["gla_h"], dims["hg_h"]
    sl = _slab_table(dims)
    h = _rmsnorm(x, norm_mix[l], BF16)
    proj = _in_proj(h, sw["wt"], sw["wide"], layer=l, d_in=sw["d_in"], slab=True, tm=2048, tn=MXU_DIM, name="in_proj")
    sm = _matmul([h], sw["w_small"], layer=l, w_transposed=True, tm=1024, tn=LANES, name="in_proj_small")
    proj4 = proj.reshape(proj.shape[0], batch, seq, LANES)
    sm3 = sm.reshape(batch, seq, LANES)
    cw = lw["dn_cw"].transpose(1, 0, 2)
    o_dn, s_dn = _dn_prompt(proj4, sm3, cw, lw["alog_vec"], lw["dtb_vec"], dn_norm[l].reshape(1, LANES),
                            batch=batch, n_heads=dn_h, slab_q=sl["dn_qkv"], slab_g=sl["dn_g"])
    o_gla, s_gla = _gla_prompt(proj4, mode="gla", layer=l, batch=batch, n_heads=gla_h,
                               slabs=(sl["gla_q"], sl["gla_k"], sl["gla_v"], sl["gla_g"]),
                               sm=sm3, wg=lw["gla_wg_heads"], bias=lw["gla_bias"].reshape(gla_h, 1, LANES),
                               gain=gla_norm[l].reshape(1, LANES))
    o_hg, s_hg = _gla_prompt(proj4, mode="hg", layer=l, batch=batch, n_heads=hg_h,
                             slabs=(sl["hg_q"], sl["hg_f"], sl["hg_i"], sl["hg_g"]),
                             lbl=hg_lbl, gain=hg_norm[l].reshape(1, LANES))
    o_list = [o_dn.reshape(m, -1), o_gla.reshape(m, -1), o_hg.reshape(m, -1)]
    x = _matmul(o_list, sw["wo"], layer=l, res=x, tm=1024, tn=512, name="out_proj")
    h2 = _rmsnorm(x, norm_ffn[l], BF16)
    hidden, tail = _ffn_up_prompt(h2, sw["ffn_wg"], sw["ffn_wu"], ffn_conv_w, sw["ffn_cb"],
                                  layer=l, batch=batch, tn=MXU_DIM)
    x = _matmul([hidden], sw["ffn_wd"], layer=l, res=x, tm=512, tn=512, name="ffn_down")
    n_qkv = 3 * dn_h
    conv_dn = proj4[sl["dn_qkv"]:sl["dn_qkv"] + n_qkv, :, seq - (DN_CONV - 1):, :]
    conv_dn = conv_dn.transpose(1, 2, 0, 3).reshape(batch, DN_CONV - 1, n_qkv * LANES)
    conv_ffn = tail[:, SUBLANES - (FFN_CONV - 1):, :]
    return x, (conv_dn, s_dn, s_gla, s_hg, conv_ffn)


def _decode_layer(x, l, lw, sw, ffn_conv_w, st_dn_conv, st_dn, st_gla, st_hg, st_ffn, norm_mix, dn_norm, gla_norm,
                  hg_norm, hg_lbl, norm_ffn, dims):
    m, d = x.shape
    depth = st_dn.shape[0]
    dn_h, gla_h, hg_h = dims["dn_h"], dims["gla_h"], dims["hg_h"]
    sl = _slab_table(dims)
    bb = SUBLANES
    h = _rmsnorm(x, norm_mix[l], BF16)
    proj = _in_proj(h, sw["wt"], sw["wide"], layer=l, d_in=sw["d_in"], slab=False, tm=m, tn=MXU_DIM,
                    name="in_proj_dec")
    sm = _matmul([h], sw["w_small"], layer=l, w_transposed=True, tm=m, tn=LANES, name="in_proj_small_dec")
    proj3 = proj.reshape(m, -1, LANES)
    conv_in = st_dn_conv.reshape(depth, m, DN_CONV - 1, 3 * dn_h, LANES)
    o_dn, conv_dn, s_dn = _dn_decode(proj3, sm, conv_in, st_dn, lw["dn_cw"], lw["alog_vec"], lw["dtb_vec"],
                                     dn_norm[l].reshape(1, LANES), layer=l, n_heads=dn_h, slab_g=sl["dn_g"], bb=bb)
    ga = _gla_gate_decode(sm, lw["gla_wg_full"], lw["gla_bias"]).reshape(m, gla_h, LANES)
    o_gla, s_gla = _gla_decode(proj3, st_gla, gla_norm[l].reshape(1, LANES), mode="gla", layer=l, n_heads=gla_h,
                               slabs=(sl["gla_q"], sl["gla_k"], sl["gla_v"], sl["gla_g"]), bb=bb, ga=ga)
    o_hg, s_hg = _gla_decode(proj3, st_hg, hg_norm[l].reshape(1, LANES), mode="hg", layer=l, n_heads=hg_h,
                             slabs=(sl["hg_q"], sl["hg_f"], sl["hg_i"], sl["hg_g"]), bb=bb, lbl=hg_lbl)
    o_list = [o_dn.reshape(m, -1), o_gla.reshape(m, -1), o_hg.reshape(m, -1)]
    x = _matmul(o_list, sw["wo"], layer=l, res=x, tm=m, tn=1024, name="out_proj_dec")
    h2 = _rmsnorm(x, norm_ffn[l], BF16)
    st_flat = st_ffn.reshape(depth, m, -1)
    hidden, gate = _ffn_up_decode(h2, sw["ffn_wg"], sw["ffn_wu"], st_flat, ffn_conv_w, sw["ffn_cb"],
                                  layer=l, tn=MXU_DIM)
    x = _matmul([hidden], sw["ffn_wd"], layer=l, res=x, tm=m, tn=512, name="ffn_down_dec")
    conv_ffn = jnp.stack([st_ffn[l][:, 1, :], gate], axis=1)
    return x, (conv_dn.reshape(m, DN_CONV - 1, -1), s_dn, s_gla, s_hg, conv_ffn)


def kernel(x_prompt, x_sample, state_dn_conv, state_dn, state_gla, state_hg, state_ffn_conv, norm_mix, w_in, dn_conv_w, dn_a_log, dn_dt_bias, dn_norm, gla_w_gate, gla_gate_bias, gla_norm, hg_lb_logits, hg_norm, w_out, norm_ffn, ffn_w_gate, ffn_w_up, ffn_conv_w, ffn_conv_b, ffn_w_down, norm_final):
    batch, seq, d_model = x_prompt.shape
    dec_batch, dec_seq, _ = x_sample.shape
    assert dec_seq == 1 and FFN_CONV == 3
    depth = w_in.shape[0]
    dims = dict(dn_h=state_dn.shape[2], gla_h=state_gla.shape[2], hg_h=state_hg.shape[2])
    hg_lbl = hg_lb_logits.reshape(depth, dims["hg_h"], HEAD_DIM).transpose(1, 0, 2)

    xp = x_prompt.reshape(batch * seq, d_model)
    xs = x_sample.reshape(dec_batch, d_model)
    p_states, s_states = [], []
    wt, wide, w_small = _in_proj_layout(w_in, dims)
    sw = dict(wt=wt, wide=wide, d_in=w_in.shape[2], w_small=w_small, wo=w_out, ffn_wg=ffn_w_gate, ffn_wu=ffn_w_up,
              ffn_wd=ffn_w_down.astype(BF16), ffn_cb=ffn_conv_b.reshape(depth, 1, -1))
    for l in range(depth):
        lw = _layer_weights(l, dn_conv_w, dn_a_log, dn_dt_bias, gla_w_gate, gla_gate_bias, dims)
        xp, ps = _prompt_layer(xp, l, lw, sw, ffn_conv_w, norm_mix, dn_norm, gla_norm, hg_norm, hg_lbl, norm_ffn,
                               dims, batch)
        xs, ss = _decode_layer(xs, l, lw, sw, ffn_conv_w, state_dn_conv, state_dn, state_gla, state_hg,
                               state_ffn_conv, norm_mix, dn_norm, gla_norm, hg_norm, hg_lbl, norm_ffn, dims)
        p_states.append(ps)
        s_states.append(ss)
    y_prompt = _rmsnorm(xp, norm_final, F32).reshape(batch, seq, d_model)
    y_sample = _rmsnorm(xs, norm_final, F32).reshape(dec_batch, dec_seq, d_model)
    stack = lambda states, i: jnp.stack([s[i] for s in states], axis=0)
    return ((y_prompt, y_sample) + tuple(stack(p_states, i) for i in range(5))
            + tuple(stack(s_states, i) for i in range(5)))
```

```python
import functools
import math

import jax
import jax.numpy as jnp
from jax import lax
from jax.experimental import pallas as pl
from jax.experimental.pallas import tpu as pltpu

F32 = jnp.float32
BF16 = jnp.bfloat16

HEAD_DIM = 128
DN_CONV = 4
GLA_LOWRANK = 16
GLA_NORMALIZER = 16.0
FFN_CONV = 3
EPS = 1e-6
LB_FLOOR = 1e-30

LANES = 128
SUBLANES = 8
BF16_SUBLANES = 16
MXU_DIM = 256
VMEM_LIMIT_BYTES = 56 * 1024 * 1024

CHUNK = 64
SUB = 8


def _cparams(sem):
    return pltpu.CompilerParams(dimension_semantics=sem, vmem_limit_bytes=VMEM_LIMIT_BYTES)


def _sigmoid(x):
    return 1.0 / (1.0 + jnp.exp(-x))


def _silu(x):
    return x * _sigmoid(x)


def _softplus(x):
    return jnp.maximum(x, 0.0) + jnp.log1p(jnp.exp(-jnp.abs(x)))


def _log_sigmoid(x):
    return -_softplus(-x)


def _logaddexp(a, b):
    return jnp.maximum(a, b) + jnp.log1p(jnp.exp(-jnp.abs(a - b)))


def _dot(a, b):
    return jnp.dot(a, b, preferred_element_type=F32)


def _as_bf16(w):
    return w if w.dtype == BF16 else w.astype(BF16)


def _dot_nt(a, b, precision=None):
    return lax.dot_general(a, b, (((1,), (1,)), ((), ())), preferred_element_type=F32, precision=precision)


def _dot_tn(a, b):
    return lax.dot_general(a, b, (((0,), (0,)), ((), ())), preferred_element_type=F32)


def _tri_incl(n):
    r = lax.broadcasted_iota(jnp.int32, (n, n), 0)
    c = lax.broadcasted_iota(jnp.int32, (n, n), 1)
    return r, c


def _gated_head_norm(o, gate, gain):
    o = o * lax.rsqrt(jnp.mean(o * o, axis=-1, keepdims=True) + EPS) * gain
    return o * _silu(gate)


def _rmsnorm_kernel(x_ref, w_ref, o_ref):
    x = x_ref[...]
    y = x * lax.rsqrt(jnp.mean(x * x, axis=-1, keepdims=True) + EPS)
    o_ref[...] = (y * w_ref[...]).astype(o_ref.dtype)


def _rmsnorm(x, w, out_dtype):
    m, d = x.shape
    tm = min(m, 256)
    return pl.pallas_call(
        _rmsnorm_kernel,
        out_shape=jax.ShapeDtypeStruct((m, d), out_dtype),
        grid=(m // tm,),
        in_specs=[pl.BlockSpec((tm, d), lambda i: (i, 0)), pl.BlockSpec((1, d), lambda i: (0, 0))],
        out_specs=pl.BlockSpec((tm, d), lambda i: (i, 0)),
        compiler_params=_cparams(("parallel",)),
        name="rmsnorm",
    )(x, w.reshape(1, d))


def _mm_fullk_kernel(*refs, n_lhs, has_res, slab, w_transposed):
    lhs = refs[:n_lhs]
    w_ref = refs[n_lhs]
    res_ref = refs[n_lhs + 1] if has_res else None
    o_ref = refs[n_lhs + 1 + (1 if has_res else 0)]
    acc, row = None, 0
    for a in lhs:
        k = a.shape[1]
        if w_transposed:
            part = _dot_nt(a[...], _as_bf16(w_ref[:, row:row + k]))
        else:
            part = _dot(a[...], _as_bf16(w_ref[row:row + k, :]))
        acc = part if acc is None else acc + part
        row += k
    if has_res:
        acc = res_ref[...] + acc
    if slab:
        for j in range(o_ref.shape[0]):
            o_ref[j] = acc[:, j * LANES:(j + 1) * LANES]
    else:
        o_ref[...] = acc.astype(o_ref.dtype)


def _matmul(lhs_list, w, *, layer, res=None, slab=False, w_transposed=False, tm, tn, name):
    m = lhs_list[0].shape[0]
    kdim, n = (w.shape[2], w.shape[1]) if w_transposed else (w.shape[1], w.shape[2])
    tm = min(tm, m)
    tn = min(tn, n)
    assert sum(a.shape[1] for a in lhs_list) == kdim
    in_specs = [pl.BlockSpec((tm, a.shape[1]), lambda i, j: (i, 0)) for a in lhs_list]
    if w_transposed:
        in_specs.append(pl.BlockSpec((None, tn, kdim), lambda i, j: (layer, j, 0)))
    else:
        in_specs.append(pl.BlockSpec((None, kdim, tn), lambda i, j: (layer, 0, j)))
    args = list(lhs_list) + [w]
    if res is not None:
        in_specs.append(pl.BlockSpec((tm, tn), lambda i, j: (i, j)))
        args.append(res)
    if slab:
        out_shape = jax.ShapeDtypeStruct((n // LANES, m, LANES), F32)
        out_spec = pl.BlockSpec((tn // LANES, tm, LANES), lambda i, j: (j, i, 0))
    else:
        out_shape = jax.ShapeDtypeStruct((m, n), F32)
        out_spec = pl.BlockSpec((tm, tn), lambda i, j: (i, j))
    return pl.pallas_call(
        functools.partial(_mm_fullk_kernel, n_lhs=len(lhs_list), has_res=res is not None, slab=slab,
                          w_transposed=w_transposed),
        out_shape=out_shape,
        grid=(m // tm, n // tn),
        in_specs=in_specs,
        out_specs=out_spec,
        compiler_params=_cparams(("parallel", "arbitrary")),
        name=name,
    )(*args)


FFN_ROW_BLOCKS = 4


def _ffn_up_prompt_kernel(x_ref, wg_ref, wu_ref, cw_ref, cb_ref, hid_ref, tail_ref):
    wg = _as_bf16(wg_ref[...])
    wu = _as_bf16(wu_ref[...])
    cw = cw_ref[...]
    cb = cb_ref[...]
    rows = x_ref.shape[0] // FFN_ROW_BLOCKS

    def dots(r):
        x = x_ref[r * rows:(r + 1) * rows, :]
        return _dot(x, wg), _dot(x, wu)

    prev = jnp.zeros((SUBLANES, wg.shape[1]), F32)
    nxt = dots(0)
    for r in range(FFN_ROW_BLOCKS):
        gate, up = nxt
        if r + 1 < FFN_ROW_BLOCKS:
            nxt = dots(r + 1)
        cat = jnp.concatenate([prev, gate], axis=0)
        y = cw[FFN_CONV - 1:FFN_CONV] * gate
        for i in range(FFN_CONV - 1):
            back = FFN_CONV - 1 - i
            y = y + cw[i:i + 1] * cat[SUBLANES - back:SUBLANES - back + rows]
        hid_ref[r * rows:(r + 1) * rows, :] = (_silu(y + cb) * up).astype(hid_ref.dtype)
        prev = gate[rows - SUBLANES:]
    tail_ref[...] = prev


def _ffn_up_prompt(h, wg, wu, cw, cb, *, layer, batch, tn):
    m, d = h.shape
    n = wg.shape[2]
    seq = m // batch
    w_spec = pl.BlockSpec((None, d, tn), lambda b, j: (layer, 0, j))
    return pl.pallas_call(
        _ffn_up_prompt_kernel,
        out_shape=(jax.ShapeDtypeStruct((m, n), BF16), jax.ShapeDtypeStruct((batch, SUBLANES, n), F32)),
        grid=(batch, n // tn),
        in_specs=[
            pl.BlockSpec((seq, d), lambda b, j: (b, 0), pipeline_mode=pl.Buffered(1)),
            w_spec, w_spec,
            pl.BlockSpec((None, FFN_CONV, tn), lambda b, j: (layer, 0, j)),
            pl.BlockSpec((None, 1, tn), lambda b, j: (layer, 0, j)),
        ],
        out_specs=(
            pl.BlockSpec((seq, tn), lambda b, j: (b, j)),
            pl.BlockSpec((None, SUBLANES, tn), lambda b, j: (b, 0, j)),
        ),
        compiler_params=_cparams(("parallel", "parallel")),
        name="ffn_up_prompt",
    )(h, wg, wu, cw, cb)


def _ffn_up_decode_kernel(x_ref, wg_ref, wu_ref, s0_ref, s1_ref, cw_ref, cb_ref, hid_ref, gate_ref):
    x = x_ref[...]
    gate = _dot(x, _as_bf16(wg_ref[...]))
    up = _dot(x, _as_bf16(wu_ref[...]))
    cw = cw_ref[...]
    y = cw[0:1] * s0_ref[...] + cw[1:2] * s1_ref[...] + cw[2:3] * gate
    hid_ref[...] = (_silu(y + cb_ref[...]) * up).astype(hid_ref.dtype)
    gate_ref[...] = gate


def _ffn_up_decode(h, wg, wu, state, cw, cb, *, layer, tn):
    m, d = h.shape
    n = wg.shape[2]
    nt = n // tn
    w_spec = pl.BlockSpec((None, d, tn), lambda j: (layer, 0, j))
    return pl.pallas_call(
        _ffn_up_decode_kernel,
        out_shape=(jax.ShapeDtypeStruct((m, n), BF16), jax.ShapeDtypeStruct((m, n), F32)),
        grid=(nt,),
        in_specs=[
            pl.BlockSpec((m, d), lambda j: (0, 0)),
            w_spec, w_spec,
            pl.BlockSpec((None, m, tn), lambda j: (layer, 0, j)),
            pl.BlockSpec((None, m, tn), lambda j: (layer, 0, nt + j)),
            pl.BlockSpec((None, FFN_CONV, tn), lambda j: (layer, 0, j)),
            pl.BlockSpec((None, 1, tn), lambda j: (layer, 0, j)),
        ],
        out_specs=(pl.BlockSpec((m, tn), lambda j: (0, j)), pl.BlockSpec((m, tn), lambda j: (0, j))),
        compiler_params=_cparams(("parallel",)),
        name="ffn_up_decode",
    )(h, wg, wu, state, state, cw, cb)


def _lane_select(x, lane_index):
    lane = lax.broadcasted_iota(jnp.int32, x.shape, 1)
    return jnp.sum(jnp.where(lane == lane_index, x, 0.0), axis=1, keepdims=True)


def _causal_conv_chunk(x_ref, cw, c, rows):
    start = pl.multiple_of(c * rows, rows)
    xc = x_ref[pl.ds(start, rows), :]
    pstart = pl.multiple_of(jnp.maximum(c * rows - SUBLANES, 0), SUBLANES)
    xp = x_ref[pl.ds(pstart, SUBLANES), :]
    xp = jnp.where(c == 0, 0.0, xp)
    cat = jnp.concatenate([xp, xc], axis=0)
    y = cw[DN_CONV - 1:DN_CONV] * xc
    for i in range(DN_CONV - 1):
        back = DN_CONV - 1 - i
        y = y + cw[i:i + 1] * cat[SUBLANES - back:SUBLANES - back + rows]
    return y


def _l2norm(x):
    return x * lax.rsqrt(jnp.sum(x * x, axis=-1, keepdims=True) + EPS)


def _split2(x):
    hi = x.astype(BF16)
    lo = (x - hi.astype(F32)).astype(BF16)
    return hi, lo


def _split3(x):
    hi = x.astype(BF16)
    r1 = x - hi.astype(F32)
    mid = r1.astype(BF16)
    lo = (r1 - mid.astype(F32)).astype(BF16)
    return hi, mid, lo


def _dot3(a, b, dot=_dot):
    a_hi, a_lo = _split2(a)
    b_hi, b_lo = _split2(b)
    return dot(a_hi, b_hi) + (dot(a_hi, b_lo) + dot(a_lo, b_hi))


def _dot2(a, b):
    b_hi, b_lo = _split2(b)
    a_b = a.astype(BF16)
    return _dot(a_b, b_hi) + _dot(a_b, b_lo)


def _cumsum_rows(tri_bf16, g):
    hi, mid, lo = _split3(g)
    return _dot(tri_bf16, hi) + (_dot(tri_bf16, mid) + _dot(tri_bf16, lo))


def _unit_lower_solve(mms, rhss):
    n = mms[0].shape[0]
    xs = [-mm for mm in mms]
    rs = [rhs + _dot2(x, rhs) for x, rhs in zip(xs, rhss)]
    for _ in range(int(math.log2(n)) - 1):
        xs = [_dot2(x, x) for x in xs]
        rs = [r + _dot2(x, r) for x, r in zip(xs, rs)]
    return rs


DN_HEADS_PER_STEP = 2
DN_CHUNKS_PER_ITER = 4


def _dn_prompt_kernel(q_ref, k_ref, v_ref, gt_ref, sm_ref, cwq_ref, cwk_ref, cwv_ref, alog_ref, dtb_ref, gain_ref,
                      o_ref, s_ref, u_s, w_s, qe_s, kd_s, attn_s, gend_s, st_s, *, n_heads):
    hp_n = q_ref.shape[0]
    head0 = pl.program_id(1) * hp_n
    seq = q_ref.shape[1]
    n_chunks = seq // CHUNK
    r, c = _tri_incl(CHUNK)
    tri = (r >= c).astype(BF16)
    incl = r >= c
    strict = r > c
    neg_a = -jnp.exp(alog_ref[...])
    dtb = dtb_ref[...]

    def phase1(it, carry):
        items = [(hp, it * DN_CHUNKS_PER_ITER + u) for hp in range(hp_n) for u in range(DN_CHUNKS_PER_ITER)]
        pre = []
        for hp, ci in items:
            rows = pl.ds(pl.multiple_of(ci * CHUNK, CHUNK), CHUNK)
            q = _silu(_causal_conv_chunk(q_ref.at[hp], cwq_ref[hp], ci, CHUNK))
            k = _silu(_causal_conv_chunk(k_ref.at[hp], cwk_ref[hp], ci, CHUNK))
            v = _silu(_causal_conv_chunk(v_ref.at[hp], cwv_ref[hp], ci, CHUNK))
            q = _l2norm(q) * (HEAD_DIM ** -0.5)
            k = _l2norm(k)
            sm = sm_ref[rows, :]
            beta = _lane_select(_sigmoid(sm), head0 + hp)
            g = _lane_select(neg_a * _softplus(sm + dtb), n_heads + head0 + hp)
            pre.append((q, k, v, beta, jnp.broadcast_to(g, (CHUNK, LANES))))
        big_gs = [_cumsum_rows(tri, p[4]) for p in pre]
        mms, rhss, gammas, egs = [], [], [], []
        for (q, k, v, beta, _), big_g in zip(pre, big_gs):
            g_row = big_g.T[:CHUNK, :]
            g_col = big_g[:, :CHUNK]
            gamma = jnp.where(incl, jnp.exp(jnp.where(incl, g_col - g_row, 0.0)), 0.0)
            kb = k.astype(BF16)
            mms.append(jnp.where(strict, beta * _dot_nt(kb, kb) * gamma, 0.0))
            eg = jnp.exp(big_g)
            rhss.append(jnp.concatenate([v * beta, k * (beta * eg)], axis=1))
            gammas.append(gamma)
            egs.append(eg)
        uws = _unit_lower_solve(mms, rhss)
        for (hp, ci), (q, k, v, beta, _), big_g, gamma, eg, uw in zip(items, pre, big_gs, gammas, egs, uws):
            rows = pl.ds(pl.multiple_of(ci * CHUNK, CHUNK), CHUNK)
            g_end = big_g[CHUNK - 1:CHUNK, :]
            u_s[hp, rows, :] = uw[:, :HEAD_DIM]
            w_s[hp, rows, :] = uw[:, HEAD_DIM:]
            qe_s[hp, rows, :] = q * eg
            kd_s[hp, rows, :] = k * jnp.exp(g_end - big_g)
            attn_s[hp, ci] = _dot_nt(q.astype(BF16), k.astype(BF16)) * gamma
            gend_s[hp, ci] = jnp.broadcast_to(jnp.exp(g_end), (SUBLANES, LANES))
        return carry

    lax.fori_loop(0, n_chunks // DN_CHUNKS_PER_ITER, phase1, 0)

    st_s[...] = jnp.zeros_like(st_s)
    gain = gain_ref[...]

    def phase2(ci, carry):
        rows = pl.ds(pl.multiple_of(ci * CHUNK, CHUNK), CHUNK)
        ss = [st_s[hp] for hp in range(hp_n)]
        sbs = [s.astype(BF16) for s in ss]
        vbs = [(u_s[hp, rows, :] - _dot(w_s[hp, rows, :].astype(BF16), sbs[hp])).astype(BF16) for hp in range(hp_n)]
        for hp in range(hp_n):
            o = _dot(qe_s[hp, rows, :].astype(BF16), sbs[hp]) + _dot(attn_s[hp, ci].astype(BF16), vbs[hp])
            st_s[hp] = ss[hp] * gend_s[hp, ci][0:1, :] + _dot_tn(kd_s[hp, rows, :].astype(BF16), vbs[hp])
            o_ref[rows, hp * LANES:(hp + 1) * LANES] = _gated_head_norm(o, gt_ref[hp, rows, :], gain).astype(o_ref.dtype)
        return carry

    lax.fori_loop(0, n_chunks, phase2, 0)
    s_ref[...] = st_s[...]


def _dn_prompt(proj, sm, cw, alog_vec, dtb_vec, gain, *, batch, n_heads, slab_q, slab_g):
    seq = proj.shape[2]
    n_chunks = seq // CHUNK
    hp = math.gcd(DN_HEADS_PER_STEP, n_heads)
    assert slab_q % hp == 0 and slab_g % hp == 0 and n_chunks % DN_CHUNKS_PER_ITER == 0

    def slab_spec(first):
        return pl.BlockSpec((hp, None, seq, LANES), lambda b, j: (first // hp + j, b, 0, 0))

    def cw_spec(first):
        return pl.BlockSpec((hp, DN_CONV, LANES), lambda b, j: (first // hp + j, 0, 0))

    vec_spec = pl.BlockSpec((1, LANES), lambda b, j: (0, 0))
    return pl.pallas_call(
        functools.partial(_dn_prompt_kernel, n_heads=n_heads),
        out_shape=(jax.ShapeDtypeStruct((batch, seq, n_heads * LANES), BF16),
                   jax.ShapeDtypeStruct((batch, n_heads, HEAD_DIM, HEAD_DIM), F32)),
        grid=(batch, n_heads // hp),
        in_specs=[
            slab_spec(slab_q), slab_spec(slab_q + n_heads), slab_spec(slab_q + 2 * n_heads), slab_spec(slab_g),
            pl.BlockSpec((None, seq, LANES), lambda b, j: (b, 0, 0)),
            cw_spec(0), cw_spec(n_heads), cw_spec(2 * n_heads),
            vec_spec, vec_spec, vec_spec,
        ],
        out_specs=(pl.BlockSpec((None, seq, hp * LANES), lambda b, j: (b, 0, j)),
                   pl.BlockSpec((None, hp, HEAD_DIM, HEAD_DIM), lambda b, j: (b, j, 0, 0))),
        scratch_shapes=[pltpu.VMEM((hp, seq, LANES), F32)] * 4 + [
            pltpu.VMEM((hp, n_chunks, CHUNK, CHUNK), F32),
            pltpu.VMEM((hp, n_chunks, SUBLANES, LANES), F32),
            pltpu.VMEM((hp, HEAD_DIM, HEAD_DIM), F32),
        ],
        compiler_params=_cparams(("parallel", "parallel")),
        name="dn_prompt",
    )(proj, proj, proj, proj, sm, cw, cw, cw, alog_vec, dtb_vec, gain)


def _hg_lower_bound(logits, layer):
    mx = jnp.max(logits, axis=0, keepdims=True)
    e = jnp.exp(logits - mx)
    p = e / jnp.sum(e, axis=0, keepdims=True)
    lb = jnp.zeros((1, logits.shape[1]), F32)
    for i in range(1, layer + 1):
        lb = lb + p[i:i + 1]
    return lb


def _hg_gates(zf, lb):
    e = jnp.exp(-jnp.abs(zf))
    log_sig = jnp.minimum(zf, 0.0) - jnp.log1p(e)
    log_f = _logaddexp(jnp.log(jnp.maximum(lb, LB_FLOOR)), jnp.log1p(-lb) + log_sig)
    k = (1.0 - lb) * (jnp.where(zf >= 0.0, e, 1.0) / (1.0 + e))
    return log_f, k


def _gla_intra_off(q, k, big_g):
    off = []
    for lo in range(SUB, q.shape[0], SUB):
        ref = big_g[lo - 1:lo]
        qt = q[lo:lo + SUB] * jnp.exp(big_g[lo:lo + SUB] - ref)
        kt = k[:lo] * jnp.exp(ref - big_g[:lo])
        off.append(_dot3(qt, kt, _dot_nt))
    return off


def _gla_intra_diag(q, k, big_g):
    n = q.shape[0]
    lane = lax.broadcasted_iota(jnp.int32, (SUB, n), 1)
    diag = []
    for lo in range(0, n, SUB):
        qi = q[lo:lo + SUB]
        gi = big_g[lo:lo + SUB]
        ki = k[lo:lo + SUB]
        blk = jnp.zeros((SUB, n), F32)
        for j in range(SUB):
            dec = jnp.exp(jnp.minimum(gi - gi[j:j + 1], 0.0))
            col = jnp.sum(qi * ki[j:j + 1] * dec, axis=1, keepdims=True)
            blk = jnp.where(lane == lo + j, col, blk)
        diag.append(blk)
    return diag


def _gla_intra_merge(off, diag):
    n = diag[0].shape[1]
    rows = [diag[0]]
    for bi, o in enumerate(off, start=1):
        rows.append(diag[bi] + jnp.concatenate([o, jnp.zeros((SUB, n - bi * SUB), F32)], axis=1))
    r, c = _tri_incl(n)
    return jnp.where(r >= c, jnp.concatenate(rows, axis=0), 0.0)


GLA_CHUNKS_PER_ITER = 8


def _gla_prompt_kernel(*refs, mode, layer):
    if mode == "gla":
        (q_ref, k_ref, v_ref, gt_ref, sm_ref, wg_ref, bias_ref, gain_ref, o_ref, s_ref,
         st_s, a_s, qe_s, kd_s, de_s) = refs
    else:
        (q_ref, k_ref, v_ref, gt_ref, lbl_ref, gain_ref, o_ref, s_ref, st_s, a_s, qe_s, kd_s, de_s) = refs
        lb = _hg_lower_bound(lbl_ref[...], layer)
    seq = q_ref.shape[0]
    n_groups = seq // (CHUNK * GLA_CHUNKS_PER_ITER)
    r, c = _tri_incl(CHUNK)
    tri = (r >= c).astype(BF16)
    gain = gain_ref[...]
    st_s[...] = jnp.zeros_like(st_s)

    def group_rows(it):
        return [pl.ds(pl.multiple_of((it * GLA_CHUNKS_PER_ITER + u) * CHUNK, CHUNK), CHUNK)
                for u in range(GLA_CHUNKS_PER_ITER)]

    def front(it):
        all_rows, pre = group_rows(it), []
        for rows in all_rows:
            if mode == "gla":
                q = q_ref[rows, :] * (HEAD_DIM ** -0.5)
                k = k_ref[rows, :]
                z = _dot(sm_ref[rows, :].astype(BF16), wg_ref[...]) + bias_ref[...]
                g = _log_sigmoid(z) * (1.0 / GLA_NORMALIZER)
            else:
                q = _silu(q_ref[rows, :])
                g, k = _hg_gates(k_ref[rows, :], lb)
            pre.append((q, k, g))
        big_gs = [_cumsum_rows(tri, g) for _, _, g in pre]
        offs = [_gla_intra_off(q, k, big_g) for (q, k, _), big_g in zip(pre, big_gs)]
        diags = [_gla_intra_diag(q, k, big_g) for (q, k, _), big_g in zip(pre, big_gs)]
        for u, ((q, k, _), big_g, off, diag) in enumerate(zip(pre, big_gs, offs, diags)):
            g_end = big_g[CHUNK - 1:CHUNK, :]
            a_s[u] = _gla_intra_merge(off, diag).astype(BF16)
            qe_s[u] = (q * jnp.exp(big_g)).astype(BF16)
            kd_s[u] = (k * jnp.exp(g_end - big_g)).astype(BF16)
            de_s[u] = jnp.broadcast_to(jnp.exp(g_end), (SUBLANES, LANES))

    def back(it):
        all_rows = group_rows(it)
        vbs = [v_ref[rows, :].astype(BF16) for rows in all_rows]
        intra = [_dot(a_s[u], vb) for u, vb in enumerate(vbs)]
        incr = [_dot_tn(vb, kd_s[u]) for u, vb in enumerate(vbs)]
        states = [st_s[...]]
        for u, t in enumerate(incr):
            states.append(states[-1] * de_s[u][0:1, :] + t)
        for u, (rows, st, av) in enumerate(zip(all_rows, states, intra)):
            o = _dot_nt(qe_s[u], st.astype(BF16)) + av
            o_ref[rows, :] = _gated_head_norm(o, gt_ref[rows, :], gain).astype(o_ref.dtype)
        st_s[...] = states[-1]

    front(0)

    def body(it, carry):
        back(it - 1)
        front(it)
        return carry

    lax.fori_loop(1, n_groups, body, 0)
    back(n_groups - 1)
    s_ref[...] = st_s[...].T


def _gla_prompt(proj, *, mode, layer, batch, n_heads, slabs, sm=None, wg=None, bias=None, lbl=None, gain=None):
    seq = proj.shape[2]

    def slab_spec(first):
        return pl.BlockSpec((None, None, seq, LANES), lambda b, h: (first + h, b, 0, 0))

    in_specs = [slab_spec(s) for s in slabs]
    args = [proj] * 4
    if mode == "gla":
        in_specs += [
            pl.BlockSpec((None, seq, LANES), lambda b, h: (b, 0, 0)),
            pl.BlockSpec((None, LANES, LANES), lambda b, h: (h, 0, 0)),
            pl.BlockSpec((None, 1, LANES), lambda b, h: (h, 0, 0)),
        ]
        args += [sm, wg, bias]
    else:
        in_specs += [pl.BlockSpec((None, lbl.shape[1], LANES), lambda b, h: (h, 0, 0))]
        args += [lbl]
    in_specs.append(pl.BlockSpec((1, LANES), lambda b, h: (0, 0)))
    args.append(gain)
    return pl.pallas_call(
        functools.partial(_gla_prompt_kernel, mode=mode, layer=layer),
        out_shape=(jax.ShapeDtypeStruct((batch, seq, n_heads * LANES), BF16),
                   jax.ShapeDtypeStruct((batch, n_heads, HEAD_DIM, HEAD_DIM), F32)),
        grid=(batch, n_heads),
        in_specs=in_specs,
        out_specs=(pl.BlockSpec((None, seq, LANES), lambda b, h: (b, 0, h)),
                   pl.BlockSpec((None, None, HEAD_DIM, HEAD_DIM), lambda b, h: (b, h, 0, 0))),
        scratch_shapes=[
            pltpu.VMEM((HEAD_DIM, HEAD_DIM), F32),
            pltpu.VMEM((GLA_CHUNKS_PER_ITER, CHUNK, CHUNK), BF16),
            pltpu.VMEM((GLA_CHUNKS_PER_ITER, CHUNK, HEAD_DIM), BF16),
            pltpu.VMEM((GLA_CHUNKS_PER_ITER, CHUNK, HEAD_DIM), BF16),
            pltpu.VMEM((GLA_CHUNKS_PER_ITER, SUBLANES, LANES), F32),
        ],
        compiler_params=_cparams(("parallel", "parallel")),
        name=mode + "_prompt",
    )(*args)


def _diag_extract(row, n_rows, offset):
    r = lax.broadcasted_iota(jnp.int32, (n_rows, LANES), 0)
    l = lax.broadcasted_iota(jnp.int32, (n_rows, LANES), 1)
    return jnp.sum(jnp.where(l == r + offset, jnp.broadcast_to(row, (n_rows, LANES)), 0.0), axis=1, keepdims=True)


def _head_selector(n_heads):
    pad = -(-n_heads // BF16_SUBLANES) * BF16_SUBLANES
    r = lax.broadcasted_iota(jnp.int32, (pad, n_heads * LANES), 0)
    c = lax.broadcasted_iota(jnp.int32, (pad, n_heads * LANES), 1)
    one = (r == lax.shift_right_logical(c, int(math.log2(LANES)))).astype(BF16)
    return jnp.concatenate([one, one, one], axis=0)


def _head_columns(x, sel):
    pad = sel.shape[0] // 3
    if pad > x.shape[0]:
        x = jnp.concatenate([x, jnp.zeros((pad - x.shape[0], x.shape[1]), F32)], axis=0)
    return _dot_tn(jnp.concatenate(_split3(x), axis=0), sel)


def _dn_decode_kernel(p_ref, sm_ref, conv_ref, s_ref, cw_ref, alog_ref, dtb_ref, gain_ref, *rest,
                      n_heads, slab_g, bb):
    o_ref, conv_out_ref, s_out_ref = rest[-3:]
    nh = n_heads
    sel = _head_selector(nh)
    cw = cw_ref[...]
    neg_a = -jnp.exp(alog_ref[...])
    dtb = dtb_ref[...]
    gain = gain_ref[...]
    for b in range(bb):
        x = p_ref[b, 0:3 * nh, :]
        y = cw[DN_CONV - 1] * x
        for i in range(DN_CONV - 1):
            y = y + cw[i] * conv_ref[b, i]
        for i in range(DN_CONV - 2):
            conv_out_ref[b, i] = conv_ref[b, i + 1]
        conv_out_ref[b, DN_CONV - 2] = x
        act = _silu(y)
        q = _l2norm(act[0:nh]) * (HEAD_DIM ** -0.5)
        k = _l2norm(act[nh:2 * nh])
        v = act[2 * nh:3 * nh]
        srow = sm_ref[b:b + 1, :]
        beta = _diag_extract(_sigmoid(srow), nh, 0)
        g = _diag_extract(neg_a * _softplus(srow + dtb), nh, nh)
        eg = jnp.exp(g)
        k_c = _head_columns(k, sel)
        q_c = _head_columns(q, sel)
        w_c = _head_columns(k * (beta * eg), sel)
        u = v * beta
        outs = []
        for h in range(nh):
            cols = slice(h * LANES, (h + 1) * LANES)
            s = s_ref[b, h]
            v_new = u[h:h + 1] - jnp.sum(w_c[:, cols] * s, axis=0, keepdims=True)
            s_new = s * eg[h:h + 1] + k_c[:, cols] * v_new
            s_out_ref[b, h] = s_new
            outs.append(jnp.sum(q_c[:, cols] * s_new, axis=0, keepdims=True))
        o = jnp.concatenate(outs, axis=0)
        gate = p_ref[b, slab_g:slab_g + nh, :]
        o_ref[b] = _gated_head_norm(o, gate, gain).astype(o_ref.dtype)


def _stacked_state_out(s_state, s_blk, layer, s_prev, n_inputs, out_index):
    shape = jax.ShapeDtypeStruct(s_state.shape, F32)
    spec = pl.BlockSpec((None,) + s_blk, lambda i: (layer, i, 0, 0, 0))
    if s_prev is None:
        return shape, spec, [], [], {}
    return shape, spec, [pl.BlockSpec(memory_space=pl.ANY)], [s_prev], {n_inputs: out_index}


def _dn_decode(proj, sm, conv_state, s_state, cw, alog_vec, dtb_vec, gain, *, layer, n_heads, slab_g, bb, s_prev):
    batch, n_slabs, _ = proj.shape
    conv_blk = (bb,) + conv_state.shape[2:]
    s_blk = (bb,) + s_state.shape[2:]
    vec_spec = pl.BlockSpec((1, LANES), lambda i: (0, 0))
    args = [proj, sm, conv_state, s_state, cw, alog_vec, dtb_vec, gain]
    s_shape, s_spec, extra_specs, extra_args, aliases = _stacked_state_out(s_state, s_blk, layer, s_prev, len(args), 2)
    return pl.pallas_call(
        functools.partial(_dn_decode_kernel, n_heads=n_heads, slab_g=slab_g, bb=bb),
        out_shape=(jax.ShapeDtypeStruct((batch, n_heads, LANES), BF16),
                   jax.ShapeDtypeStruct(conv_state.shape[1:], F32),
                   s_shape),
        grid=(batch // bb,),
        in_specs=[
            pl.BlockSpec((bb, n_slabs, LANES), lambda i: (i, 0, 0)),
            pl.BlockSpec((bb, LANES), lambda i: (i, 0)),
            pl.BlockSpec((None,) + conv_blk, lambda i: (layer, i, 0, 0, 0)),
            pl.BlockSpec((None,) + s_blk, lambda i: (layer, i, 0, 0, 0)),
            pl.BlockSpec(cw.shape, lambda i: (0, 0, 0)),
            vec_spec, vec_spec, vec_spec,
        ] + extra_specs,
        out_specs=(pl.BlockSpec((bb, n_heads, LANES), lambda i: (i, 0, 0)),
                   pl.BlockSpec(conv_blk, lambda i: (i, 0, 0, 0)),
                   s_spec),
        input_output_aliases=aliases,
        compiler_params=_cparams(("parallel",)),
        name="dn_decode",
    )(*args, *extra_args)


def _gla_gate_decode_kernel(sm_ref, wg_ref, bias_ref, o_ref):
    z = _dot(sm_ref[...].astype(BF16), wg_ref[...]) + bias_ref[...]
    o_ref[...] = _log_sigmoid(z) * (1.0 / GLA_NORMALIZER)


def _gla_gate_decode(sm, wg_full, bias):
    m = sm.shape[0]
    n = wg_full.shape[1]
    return pl.pallas_call(
        _gla_gate_decode_kernel,
        out_shape=jax.ShapeDtypeStruct((m, n), F32),
        name="gla_gate_decode",
    )(sm, wg_full, bias)


def _gla_decode_kernel(*refs, mode, layer, n_heads, slabs, bb):
    o_ref, s_out_ref = refs[-2:]
    if mode == "gla":
        p_ref, ga_ref, s_ref, gain_ref = refs[:4]
    else:
        p_ref, lbl_ref, s_ref, gain_ref = refs[:4]
    nh = n_heads
    sel = _head_selector(nh)
    sq, sk, sv, sg = slabs
    gain = gain_ref[...]
    if mode == "hg":
        lb = jnp.concatenate([_hg_lower_bound(lbl_ref[h], layer) for h in range(nh)], axis=0)
    for b in range(bb):
        if mode == "gla":
            q = p_ref[b, sq:sq + nh, :] * (HEAD_DIM ** -0.5)
            k = p_ref[b, sk:sk + nh, :]
            g = ga_ref[b]
        else:
            q = _silu(p_ref[b, sq:sq + nh, :])
            g, k = _hg_gates(p_ref[b, sk:sk + nh, :], lb)
        v = p_ref[b, sv:sv + nh, :]
        a_c = _head_columns(jnp.exp(g), sel)
        k_c = _head_columns(k, sel)
        q_c = _head_columns(q, sel)
        outs = []
        for h in range(nh):
            cols = slice(h * LANES, (h + 1) * LANES)
            s_new = a_c[:, cols] * s_ref[b, h] + k_c[:, cols] * v[h:h + 1]
            s_out_ref[b, h] = s_new
            outs.append(jnp.sum(q_c[:, cols] * s_new, axis=0, keepdims=True))
        o = jnp.concatenate(outs, axis=0)
        o_ref[b] = _gated_head_norm(o, p_ref[b, sg:sg + nh, :], gain).astype(o_ref.dtype)


def _gla_decode(proj, s_state, gain, *, mode, layer, n_heads, slabs, bb, s_prev, ga=None, lbl=None):
    batch, n_slabs, _ = proj.shape
    s_blk = (bb,) + s_state.shape[2:]
    if mode == "gla":
        extra, extra_spec = ga, pl.BlockSpec((bb, n_heads, LANES), lambda i: (i, 0, 0))
    else:
        extra, extra_spec = lbl, pl.BlockSpec(lbl.shape, lambda i: (0, 0, 0))
    args = [proj, extra, s_state, gain]
    s_shape, s_spec, extra_specs, extra_args, aliases = _stacked_state_out(s_state, s_blk, layer, s_prev, len(args), 1)
    return pl.pallas_call(
        functools.partial(_gla_decode_kernel, mode=mode, layer=layer, n_heads=n_heads, slabs=slabs, bb=bb),
        out_shape=(jax.ShapeDtypeStruct((batch, n_heads, LANES), BF16), s_shape),
        grid=(batch // bb,),
        in_specs=[
            pl.BlockSpec((bb, n_slabs, LANES), lambda i: (i, 0, 0)),
            extra_spec,
            pl.BlockSpec((None,) + s_blk, lambda i: (layer, i, 0, 0, 0)),
            pl.BlockSpec((1, LANES), lambda i: (0, 0)),
        ] + extra_specs,
        out_specs=(pl.BlockSpec((bb, n_heads, LANES), lambda i: (i, 0, 0)), s_spec),
        input_output_aliases=aliases,
        compiler_params=_cparams(("parallel",)),
        name=mode + "_decode",
    )(*args, *extra_args)


def _in_proj_layout(w_in, dims):
    dn_h, gla_h, hg_h = dims["dn_h"], dims["gla_h"], dims["hg_h"]
    dn_dim, gla_dim, hg_dim = dn_h * HEAD_DIM, gla_h * HEAD_DIM, hg_h * HEAD_DIM
    sizes = (3 * dn_dim, dn_h, dn_h, dn_dim, gla_dim, gla_dim, gla_dim, GLA_LOWRANK, gla_dim,
             hg_dim, hg_dim, hg_dim, hg_dim)
    offs = [0]
    for s in sizes:
        offs.append(offs[-1] + s)
    wide = ((offs[0], offs[1]), (offs[3], offs[7]), (offs[8], offs[13]))
    narrow = ((offs[1], offs[3]), (offs[7], offs[8]))
    depth, d, d_in = w_in.shape
    wt = jnp.swapaxes(w_in, 1, 2)
    used = sum(b - a for a, b in narrow)
    small = jnp.concatenate([wt[:, a:b] for a, b in narrow], axis=1)
    small = jnp.pad(small, ((0, 0), (0, LANES - used), (0, 0)))
    return wt.reshape(depth * d_in, d), wide, small


def _in_proj_kernel(x_ref, w_ref, o_ref, *, slab):
    acc = _dot_nt(x_ref[...], _as_bf16(w_ref[...]))
    if slab:
        for j in range(o_ref.shape[0]):
            o_ref[j] = acc[:, j * LANES:(j + 1) * LANES]
    else:
        o_ref[...] = acc


def _in_proj(h, wt, wide, *, layer, d_in, slab, tm, tn, name):
    m, d = h.shape
    n = sum(b - a for a, b in wide)
    tm = min(tm, m)
    starts, pos = [], 0
    for a, b in wide:
        assert (b - a) % tn == 0 and a % SUBLANES == 0
        starts.append((pos, a - pos))
        pos += b - a

    def w_index(i, j):
        col = j * tn
        shift = starts[0][1]
        for first, sh in starts[1:]:
            shift = jnp.where(col >= first, sh, shift)
        return (pl.multiple_of(layer * d_in + col + shift, SUBLANES), 0)

    if slab:
        out_shape = jax.ShapeDtypeStruct((n // LANES, m, LANES), F32)
        out_spec = pl.BlockSpec((tn // LANES, tm, LANES), lambda i, j: (j, i, 0))
    else:
        out_shape = jax.ShapeDtypeStruct((m, n), F32)
        out_spec = pl.BlockSpec((tm, tn), lambda i, j: (i, j))
    return pl.pallas_call(
        functools.partial(_in_proj_kernel, slab=slab),
        out_shape=out_shape,
        grid=(m // tm, n // tn),
        in_specs=[pl.BlockSpec((tm, d), lambda i, j: (i, 0), pipeline_mode=pl.Buffered(1)),
                  pl.BlockSpec((pl.Element(tn), pl.Element(d)), w_index)],
        out_specs=out_spec,
        compiler_params=_cparams(("parallel", "arbitrary")),
        name=name,
    )(h, wt)


def _layer_weights(l, dn_conv_w, dn_a_log, dn_dt_bias, gla_w_gate, gla_gate_bias, dims):
    dn_h, gla_h = dims["dn_h"], dims["gla_h"]
    gla_dim = gla_h * HEAD_DIM
    lr0 = 2 * dn_h
    wg_rows = jnp.pad(gla_w_gate[l], ((lr0, LANES - lr0 - GLA_LOWRANK), (0, 0))).astype(BF16)
    return dict(
        dn_cw=dn_conv_w[l].reshape(DN_CONV, 3 * dn_h, HEAD_DIM),
        alog_vec=jnp.pad(dn_a_log[l], (dn_h, LANES - 2 * dn_h)).reshape(1, LANES),
        dtb_vec=jnp.pad(dn_dt_bias[l], (dn_h, LANES - 2 * dn_h)).reshape(1, LANES),
        gla_wg_full=wg_rows,
        gla_wg_heads=wg_rows.reshape(LANES, gla_h, HEAD_DIM).transpose(1, 0, 2),
        gla_bias=gla_gate_bias[l].reshape(1, gla_dim),
    )


def _slab_table(dims):
    dn_h, gla_h, hg_h = dims["dn_h"], dims["gla_h"], dims["hg_h"]
    names = [("dn_qkv", 3 * dn_h), ("dn_g", dn_h), ("gla_q", gla_h), ("gla_k", gla_h), ("gla_v", gla_h),
             ("gla_g", gla_h), ("hg_q", hg_h), ("hg_f", hg_h), ("hg_i", hg_h), ("hg_g", hg_h)]
    table, pos = {}, 0
    for n, cnt in names:
        table[n] = pos
        pos += cnt
    return table


def _prompt_layer(x, l, lw, sw, ffn_conv_w, norm_mix, dn_norm, gla_norm, hg_norm, hg_lbl, norm_ffn, dims, batch):
    m, d = x.shape
    seq = m // batch
    dn_h, gla_h, hg_h = dims["dn_h"], dims["gla_h"], dims["hg_h"]
    sl = _slab_table(dims)
    h = _rmsnorm(x, norm_mix[l], BF16)
    proj = _in_proj(h, sw["wt"], sw["wide"], layer=l, d_in=sw["d_in"], slab=True, tm=2048, tn=MXU_DIM, name="in_proj")
    sm = _matmul([h], sw["w_small"], layer=l, w_transposed=True, tm=1024, tn=LANES, name="in_proj_small")
    proj4 = proj.reshape(proj.shape[0], batch, seq, LANES)
    sm3 = sm.reshape(batch, seq, LANES)
    cw = lw["dn_cw"].transpose(1, 0, 2)
    o_dn, s_dn = _dn_prompt(proj4, sm3, cw, lw["alog_vec"], lw["dtb_vec"], dn_norm[l].reshape(1, LANES),
                            batch=batch, n_heads=dn_h, slab_q=sl["dn_qkv"], slab_g=sl["dn_g"])
    o_gla, s_gla = _gla_prompt(proj4, mode="gla", layer=l, batch=batch, n_heads=gla_h,
                               slabs=(sl["gla_q"], sl["gla_k"], sl["gla_v"], sl["gla_g"]),
                               sm=sm3, wg=lw["gla_wg_heads"], bias=lw["gla_bias"].reshape(gla_h, 1, LANES),
                               gain=gla_norm[l].reshape(1, LANES))
    o_hg, s_hg = _gla_prompt(proj4, mode="hg", layer=l, batch=batch, n_heads=hg_h,
                             slabs=(sl["hg_q"], sl["hg_f"], sl["hg_i"], sl["hg_g"]),
                             lbl=hg_lbl, gain=hg_norm[l].reshape(1, LANES))
    o_list = [o_dn.reshape(m, -1), o_gla.reshape(m, -1), o_hg.reshape(m, -1)]
    x = _matmul(o_list, sw["wo"], layer=l, res=x, tm=1024, tn=512, name="out_proj")
    h2 = _rmsnorm(x, norm_ffn[l], BF16)
    hidden, tail = _ffn_up_prompt(h2, sw["ffn_wg"], sw["ffn_wu"], ffn_conv_w, sw["ffn_cb"],
                                  layer=l, batch=batch, tn=MXU_DIM)
    x = _matmul([hidden], sw["ffn_wd"], layer=l, res=x, tm=512, tn=512, name="ffn_down")
    n_qkv = 3 * dn_h
    conv_dn = proj4[sl["dn_qkv"]:sl["dn_qkv"] + n_qkv, :, seq - (DN_CONV - 1):, :]
    conv_dn = conv_dn.transpose(1, 2, 0, 3).reshape(batch, DN_CONV - 1, n_qkv * LANES)
    conv_ffn = tail[:, SUBLANES - (FFN_CONV - 1):, :]
    return x, (conv_dn, s_dn, s_gla, s_hg, conv_ffn)


def _decode_layer(x, l, lw, sw, ffn_conv_w, st_dn_conv, st_dn, st_gla, st_hg, st_ffn, norm_mix, dn_norm, gla_norm,
                  hg_norm, hg_lbl, norm_ffn, dims, prev):
    m, d = x.shape
    depth = st_dn.shape[0]
    dn_h, gla_h, hg_h = dims["dn_h"], dims["gla_h"], dims["hg_h"]
    sl = _slab_table(dims)
    bb = SUBLANES
    h = _rmsnorm(x, norm_mix[l], BF16)
    proj = _in_proj(h, sw["wt"], sw["wide"], layer=l, d_in=sw["d_in"], slab=False, tm=m, tn=MXU_DIM,
                    name="in_proj_dec")
    sm = _matmul([h], sw["w_small"], layer=l, w_transposed=True, tm=m, tn=LANES, name="in_proj_small_dec")
    proj3 = proj.reshape(m, -1, LANES)
    conv_in = st_dn_conv.reshape(depth, m, DN_CONV - 1, 3 * dn_h, LANES)
    o_dn, conv_dn, s_dn = _dn_decode(proj3, sm, conv_in, st_dn, lw["dn_cw"], lw["alog_vec"], lw["dtb_vec"],
                                     dn_norm[l].reshape(1, LANES), layer=l, n_heads=dn_h, slab_g=sl["dn_g"], bb=bb,
                                     s_prev=prev[0])
    ga = _gla_gate_decode(sm, lw["gla_wg_full"], lw["gla_bias"]).reshape(m, gla_h, LANES)
    o_gla, s_gla = _gla_decode(proj3, st_gla, gla_norm[l].reshape(1, LANES), mode="gla", layer=l, n_heads=gla_h,
                               slabs=(sl["gla_q"], sl["gla_k"], sl["gla_v"], sl["gla_g"]), bb=bb, s_prev=prev[1],
                               ga=ga)
    o_hg, s_hg = _gla_decode(proj3, st_hg, hg_norm[l].reshape(1, LANES), mode="hg", layer=l, n_heads=hg_h,
                             slabs=(sl["hg_q"], sl["hg_f"], sl["hg_i"], sl["hg_g"]), bb=bb, s_prev=prev[2],
                             lbl=hg_lbl)
    o_list = [o_dn.reshape(m, -1), o_gla.reshape(m, -1), o_hg.reshape(m, -1)]
    x = _matmul(o_list, sw["wo"], layer=l, res=x, tm=m, tn=1024, name="out_proj_dec")
    h2 = _rmsnorm(x, norm_ffn[l], BF16)
    st_flat = st_ffn.reshape(depth, m, -1)
    hidden, gate = _ffn_up_decode(h2, sw["ffn_wg"], sw["ffn_wu"], st_flat, ffn_conv_w, sw["ffn_cb"],
                                  layer=l, tn=MXU_DIM)
    x = _matmul([hidden], sw["ffn_wd"], layer=l, res=x, tm=m, tn=512, name="ffn_down_dec")
    conv_ffn = jnp.stack([st_ffn[l][:, 1, :], gate], axis=1)
    return x, (conv_dn.reshape(m, DN_CONV - 1, -1), conv_ffn), (s_dn, s_gla, s_hg)


def kernel(x_prompt, x_sample, state_dn_conv, state_dn, state_gla, state_hg, state_ffn_conv, norm_mix, w_in, dn_conv_w, dn_a_log, dn_dt_bias, dn_norm, gla_w_gate, gla_gate_bias, gla_norm, hg_lb_logits, hg_norm, w_out, norm_ffn, ffn_w_gate, ffn_w_up, ffn_conv_w, ffn_conv_b, ffn_w_down, norm_final):
    batch, seq, d_model = x_prompt.shape
    dec_batch, dec_seq, _ = x_sample.shape
    assert dec_seq == 1 and FFN_CONV == 3
    depth = w_in.shape[0]
    dims = dict(dn_h=state_dn.shape[2], gla_h=state_gla.shape[2], hg_h=state_hg.shape[2])
    hg_lbl = hg_lb_logits.reshape(depth, dims["hg_h"], HEAD_DIM).transpose(1, 0, 2)

    xp = x_prompt.reshape(batch * seq, d_model)
    xs = x_sample.reshape(dec_batch, d_model)
    p_states, s_states = [], []
    wt, wide, w_small = _in_proj_layout(w_in, dims)
    sw = dict(wt=wt, wide=wide, d_in=w_in.shape[2], w_small=w_small, wo=w_out, ffn_wg=ffn_w_gate, ffn_wu=ffn_w_up,
              ffn_wd=ffn_w_down.astype(BF16), ffn_cb=ffn_conv_b.reshape(depth, 1, -1))
    s_mats = (None, None, None)
    for l in range(depth):
        lw = _layer_weights(l, dn_conv_w, dn_a_log, dn_dt_bias, gla_w_gate, gla_gate_bias, dims)
        xp, ps = _prompt_layer(xp, l, lw, sw, ffn_conv_w, norm_mix, dn_norm, gla_norm, hg_norm, hg_lbl, norm_ffn,
                               dims, batch)
        xs, ss, s_mats = _decode_layer(xs, l, lw, sw, ffn_conv_w, state_dn_conv, state_dn, state_gla, state_hg,
                                       state_ffn_conv, norm_mix, dn_norm, gla_norm, hg_norm, hg_lbl, norm_ffn, dims,
                                       s_mats)
        p_states.append(ps)
        s_states.append(ss)
    y_prompt = _rmsnorm(xp, norm_final, F32).reshape(batch, seq, d_model)
    y_sample = _rmsnorm(xs, norm_final, F32).reshape(dec_batch, dec_seq, d_model)
    stack = lambda states, i: jnp.stack([s[i] for s in states], axis=0)
    return ((y_prompt, y_sample) + tuple(stack(p_states, i) for i in range(5))
            + (stack(s_states, 0),) + s_mats + (stack(s_states, 1),))
```

```python
import functools
import math

import jax
import jax.numpy as jnp
from jax import lax
from jax.experimental import pallas as pl
from jax.experimental.pallas import tpu as pltpu

F32 = jnp.float32
BF16 = jnp.bfloat16

HEAD_DIM = 128
DN_CONV = 4
GLA_LOWRANK = 16
GLA_NORMALIZER = 16.0
FFN_CONV = 3
EPS = 1e-6
LB_FLOOR = 1e-30

LANES = 128
SUBLANES = 8
BF16_SUBLANES = 16
MXU_DIM = 256
VMEM_LIMIT_BYTES = 56 * 1024 * 1024

CHUNK = 64
SUB = 8


def _cparams(sem):
    return pltpu.CompilerParams(dimension_semantics=sem, vmem_limit_bytes=VMEM_LIMIT_BYTES)


def _sigmoid(x):
    return 1.0 / (1.0 + jnp.exp(-x))


def _silu(x):
    return x * _sigmoid(x)


def _softplus(x):
    return jnp.maximum(x, 0.0) + jnp.log1p(jnp.exp(-jnp.abs(x)))


def _log_sigmoid(x):
    return -_softplus(-x)


def _logaddexp(a, b):
    return jnp.maximum(a, b) + jnp.log1p(jnp.exp(-jnp.abs(a - b)))


def _dot(a, b):
    return jnp.dot(a, b, preferred_element_type=F32)


def _as_bf16(w):
    return w if w.dtype == BF16 else w.astype(BF16)


def _dot_nt(a, b, precision=None):
    return lax.dot_general(a, b, (((1,), (1,)), ((), ())), preferred_element_type=F32, precision=precision)


def _dot_tn(a, b):
    return lax.dot_general(a, b, (((0,), (0,)), ((), ())), preferred_element_type=F32)


def _tri_incl(n):
    r = lax.broadcasted_iota(jnp.int32, (n, n), 0)
    c = lax.broadcasted_iota(jnp.int32, (n, n), 1)
    return r, c


def _gated_head_norm(o, gate, gain):
    o = o * lax.rsqrt(jnp.mean(o * o, axis=-1, keepdims=True) + EPS) * gain
    return o * _silu(gate)


def _rmsnorm_kernel(x_ref, w_ref, o_ref):
    x = x_ref[...]
    y = x * lax.rsqrt(jnp.mean(x * x, axis=-1, keepdims=True) + EPS)
    o_ref[...] = (y * w_ref[...]).astype(o_ref.dtype)


def _rmsnorm(x, w, out_dtype):
    m, d = x.shape
    tm = min(m, 256)
    return pl.pallas_call(
        _rmsnorm_kernel,
        out_shape=jax.ShapeDtypeStruct((m, d), out_dtype),
        grid=(m // tm,),
        in_specs=[pl.BlockSpec((tm, d), lambda i: (i, 0)), pl.BlockSpec((1, d), lambda i: (0, 0))],
        out_specs=pl.BlockSpec((tm, d), lambda i: (i, 0)),
        compiler_params=_cparams(("parallel",)),
        name="rmsnorm",
    )(x, w.reshape(1, d))


def _mm_fullk_kernel(*refs, n_lhs, has_res, slab, w_transposed):
    lhs = refs[:n_lhs]
    w_ref = refs[n_lhs]
    res_ref = refs[n_lhs + 1] if has_res else None
    o_ref = refs[n_lhs + 1 + (1 if has_res else 0)]
    acc, row = None, 0
    for a in lhs:
        k = a.shape[1]
        if w_transposed:
            part = _dot_nt(a[...], _as_bf16(w_ref[:, row:row + k]))
        else:
            part = _dot(a[...], _as_bf16(w_ref[row:row + k, :]))
        acc = part if acc is None else acc + part
        row += k
    if has_res:
        acc = res_ref[...] + acc
    if slab:
        for j in range(o_ref.shape[0]):
            o_ref[j] = acc[:, j * LANES:(j + 1) * LANES]
    else:
        o_ref[...] = acc.astype(o_ref.dtype)


def _matmul(lhs_list, w, *, layer, res=None, slab=False, w_transposed=False, tm, tn, name):
    m = lhs_list[0].shape[0]
    kdim, n = (w.shape[2], w.shape[1]) if w_transposed else (w.shape[1], w.shape[2])
    tm = min(tm, m)
    tn = min(tn, n)
    assert sum(a.shape[1] for a in lhs_list) == kdim
    in_specs = [pl.BlockSpec((tm, a.shape[1]), lambda i, j: (i, 0)) for a in lhs_list]
    if w_transposed:
        in_specs.append(pl.BlockSpec((None, tn, kdim), lambda i, j: (layer, j, 0)))
    else:
        in_specs.append(pl.BlockSpec((None, kdim, tn), lambda i, j: (layer, 0, j)))
    args = list(lhs_list) + [w]
    if res is not None:
        in_specs.append(pl.BlockSpec((tm, tn), lambda i, j: (i, j)))
        args.append(res)
    if slab:
        out_shape = jax.ShapeDtypeStruct((n // LANES, m, LANES), F32)
        out_spec = pl.BlockSpec((tn // LANES, tm, LANES), lambda i, j: (j, i, 0))
    else:
        out_shape = jax.ShapeDtypeStruct((m, n), F32)
        out_spec = pl.BlockSpec((tm, tn), lambda i, j: (i, j))
    return pl.pallas_call(
        functools.partial(_mm_fullk_kernel, n_lhs=len(lhs_list), has_res=res is not None, slab=slab,
                          w_transposed=w_transposed),
        out_shape=out_shape,
        grid=(m // tm, n // tn),
        in_specs=in_specs,
        out_specs=out_spec,
        compiler_params=_cparams(("parallel", "arbitrary")),
        name=name,
    )(*args)


FFN_ROW_BLOCKS = 4


def _ffn_up_prompt_kernel(x_ref, wg_ref, wu_ref, cw_ref, cb_ref, hid_ref, tail_ref):
    wg = _as_bf16(wg_ref[...])
    wu = _as_bf16(wu_ref[...])
    cw = cw_ref[...]
    cb = cb_ref[...]
    rows = x_ref.shape[0] // FFN_ROW_BLOCKS

    def dots(r):
        x = x_ref[r * rows:(r + 1) * rows, :]
        return _dot(x, wg), _dot(x, wu)

    prev = jnp.zeros((SUBLANES, wg.shape[1]), F32)
    nxt = dots(0)
    for r in range(FFN_ROW_BLOCKS):
        gate, up = nxt
        if r + 1 < FFN_ROW_BLOCKS:
            nxt = dots(r + 1)
        cat = jnp.concatenate([prev, gate], axis=0)
        y = cw[FFN_CONV - 1:FFN_CONV] * gate
        for i in range(FFN_CONV - 1):
            back = FFN_CONV - 1 - i
            y = y + cw[i:i + 1] * cat[SUBLANES - back:SUBLANES - back + rows]
        hid_ref[r * rows:(r + 1) * rows, :] = (_silu(y + cb) * up).astype(hid_ref.dtype)
        prev = gate[rows - SUBLANES:]
    tail_ref[...] = prev


def _ffn_up_prompt(h, wg, wu, cw, cb, *, layer, batch, tn):
    m, d = h.shape
    n = wg.shape[2]
    seq = m // batch
    w_spec = pl.BlockSpec((None, d, tn), lambda b, j: (layer, 0, j))
    return pl.pallas_call(
        _ffn_up_prompt_kernel,
        out_shape=(jax.ShapeDtypeStruct((m, n), BF16), jax.ShapeDtypeStruct((batch, SUBLANES, n), F32)),
        grid=(batch, n // tn),
        in_specs=[
            pl.BlockSpec((seq, d), lambda b, j: (b, 0), pipeline_mode=pl.Buffered(1)),
            w_spec, w_spec,
            pl.BlockSpec((None, FFN_CONV, tn), lambda b, j: (layer, 0, j)),
            pl.BlockSpec((None, 1, tn), lambda b, j: (layer, 0, j)),
        ],
        out_specs=(
            pl.BlockSpec((seq, tn), lambda b, j: (b, j)),
            pl.BlockSpec((None, SUBLANES, tn), lambda b, j: (b, 0, j)),
        ),
        compiler_params=_cparams(("parallel", "parallel")),
        name="ffn_up_prompt",
    )(h, wg, wu, cw, cb)


def _ffn_up_decode_kernel(x_ref, wg_ref, wu_ref, s0_ref, s1_ref, cw_ref, cb_ref, hid_ref, gate_ref):
    x = x_ref[...]
    gate = _dot(x, _as_bf16(wg_ref[...]))
    up = _dot(x, _as_bf16(wu_ref[...]))
    cw = cw_ref[...]
    y = cw[0:1] * s0_ref[...] + cw[1:2] * s1_ref[...] + cw[2:3] * gate
    hid_ref[...] = (_silu(y + cb_ref[...]) * up).astype(hid_ref.dtype)
    gate_ref[...] = gate


def _ffn_up_decode(h, wg, wu, state, cw, cb, *, layer, tn):
    m, d = h.shape
    n = wg.shape[2]
    nt = n // tn
    w_spec = pl.BlockSpec((None, d, tn), lambda j: (layer, 0, j))
    return pl.pallas_call(
        _ffn_up_decode_kernel,
        out_shape=(jax.ShapeDtypeStruct((m, n), BF16), jax.ShapeDtypeStruct((m, n), F32)),
        grid=(nt,),
        in_specs=[
            pl.BlockSpec((m, d), lambda j: (0, 0)),
            w_spec, w_spec,
            pl.BlockSpec((None, m, tn), lambda j: (layer, 0, j)),
            pl.BlockSpec((None, m, tn), lambda j: (layer, 0, nt + j)),
            pl.BlockSpec((None, FFN_CONV, tn), lambda j: (layer, 0, j)),
            pl.BlockSpec((None, 1, tn), lambda j: (layer, 0, j)),
        ],
        out_specs=(pl.BlockSpec((m, tn), lambda j: (0, j)), pl.BlockSpec((m, tn), lambda j: (0, j))),
        compiler_params=_cparams(("parallel",)),
        name="ffn_up_decode",
    )(h, wg, wu, state, state, cw, cb)


def _lane_select(x, lane_index):
    lane = lax.broadcasted_iota(jnp.int32, x.shape, 1)
    return jnp.sum(jnp.where(lane == lane_index, x, 0.0), axis=1, keepdims=True)


def _causal_conv_chunk(x_ref, cw, c, rows):
    start = pl.multiple_of(c * rows, rows)
    xc = x_ref[pl.ds(start, rows), :]
    pstart = pl.multiple_of(jnp.maximum(c * rows - SUBLANES, 0), SUBLANES)
    xp = x_ref[pl.ds(pstart, SUBLANES), :]
    xp = jnp.where(c == 0, 0.0, xp)
    cat = jnp.concatenate([xp, xc], axis=0)
    y = cw[DN_CONV - 1:DN_CONV] * xc
    for i in range(DN_CONV - 1):
        back = DN_CONV - 1 - i
        y = y + cw[i:i + 1] * cat[SUBLANES - back:SUBLANES - back + rows]
    return y


def _l2norm(x):
    return x * lax.rsqrt(jnp.sum(x * x, axis=-1, keepdims=True) + EPS)


def _split2(x):
    hi = x.astype(BF16)
    lo = (x - hi.astype(F32)).astype(BF16)
    return hi, lo


def _split3(x):
    hi = x.astype(BF16)
    r1 = x - hi.astype(F32)
    mid = r1.astype(BF16)
    lo = (r1 - mid.astype(F32)).astype(BF16)
    return hi, mid, lo


def _dot3(a, b, dot=_dot):
    a_hi, a_lo = _split2(a)
    b_hi, b_lo = _split2(b)
    return dot(a_hi, b_hi) + (dot(a_hi, b_lo) + dot(a_lo, b_hi))


def _dot2(a, b):
    b_hi, b_lo = _split2(b)
    a_b = a.astype(BF16)
    return _dot(a_b, b_hi) + _dot(a_b, b_lo)


def _cumsum_rows(tri_bf16, g):
    hi, mid, lo = _split3(g)
    return _dot(tri_bf16, hi) + (_dot(tri_bf16, mid) + _dot(tri_bf16, lo))


def _unit_lower_solve(mms, rhss):
    n = mms[0].shape[0]
    xs = [-mm for mm in mms]
    rs = [rhs + _dot2(x, rhs) for x, rhs in zip(xs, rhss)]
    for _ in range(int(math.log2(n)) - 1):
        xs = [_dot2(x, x) for x in xs]
        rs = [r + _dot2(x, r) for x, r in zip(xs, rs)]
    return rs


DN_HEADS_PER_STEP = 2
DN_CHUNKS_PER_ITER = 4


def _dn_prompt_kernel(q_ref, k_ref, v_ref, gt_ref, sm_ref, cwq_ref, cwk_ref, cwv_ref, alog_ref, dtb_ref, gain_ref,
                      o_ref, s_ref, u_s, w_s, qe_s, kd_s, attn_s, gend_s, st_s, sb_s, vb_s, *, n_heads):
    hp_n = q_ref.shape[0]
    head0 = pl.program_id(1) * hp_n
    seq = q_ref.shape[1]
    n_chunks = seq // CHUNK
    r, c = _tri_incl(CHUNK)
    tri = (r >= c).astype(BF16)
    incl = r >= c
    strict = r > c
    neg_a = -jnp.exp(alog_ref[...])
    dtb = dtb_ref[...]

    def phase1(it, carry):
        items = [(hp, it * DN_CHUNKS_PER_ITER + u) for hp in range(hp_n) for u in range(DN_CHUNKS_PER_ITER)]
        pre = []
        for hp, ci in items:
            rows = pl.ds(pl.multiple_of(ci * CHUNK, CHUNK), CHUNK)
            q = _silu(_causal_conv_chunk(q_ref.at[hp], cwq_ref[hp], ci, CHUNK))
            k = _silu(_causal_conv_chunk(k_ref.at[hp], cwk_ref[hp], ci, CHUNK))
            v = _silu(_causal_conv_chunk(v_ref.at[hp], cwv_ref[hp], ci, CHUNK))
            q = _l2norm(q) * (HEAD_DIM ** -0.5)
            k = _l2norm(k)
            sm = sm_ref[rows, :]
            beta = _lane_select(_sigmoid(sm), head0 + hp)
            g = _lane_select(neg_a * _softplus(sm + dtb), n_heads + head0 + hp)
            pre.append((q, k, v, beta, jnp.broadcast_to(g, (CHUNK, LANES))))
        big_gs = [_cumsum_rows(tri, p[4]) for p in pre]
        mms, rhss, gammas, egs = [], [], [], []
        for (q, k, v, beta, _), big_g in zip(pre, big_gs):
            g_row = big_g.T[:CHUNK, :]
            g_col = big_g[:, :CHUNK]
            gamma = jnp.where(incl, jnp.exp(jnp.where(incl, g_col - g_row, 0.0)), 0.0)
            kb = k.astype(BF16)
            mms.append(jnp.where(strict, beta * _dot_nt(kb, kb) * gamma, 0.0))
            eg = jnp.exp(big_g)
            rhss.append(jnp.concatenate([v * beta, k * (beta * eg)], axis=1))
            gammas.append(gamma)
            egs.append(eg)
        uws = _unit_lower_solve(mms, rhss)
        for (hp, ci), (q, k, v, beta, _), big_g, gamma, eg, uw in zip(items, pre, big_gs, gammas, egs, uws):
            rows = pl.ds(pl.multiple_of(ci * CHUNK, CHUNK), CHUNK)
            g_end = big_g[CHUNK - 1:CHUNK, :]
            u_s[hp, rows, :] = uw[:, :HEAD_DIM]
            w_s[hp, rows, :] = uw[:, HEAD_DIM:]
            qe_s[hp, rows, :] = q * eg
            kd_s[hp, rows, :] = k * jnp.exp(g_end - big_g)
            attn_s[hp, ci] = _dot_nt(q.astype(BF16), k.astype(BF16)) * gamma
            gend_s[hp, ci] = jnp.broadcast_to(jnp.exp(g_end), (SUBLANES, LANES))
        return carry

    lax.fori_loop(0, n_chunks // DN_CHUNKS_PER_ITER, phase1, 0)

    st_s[...] = jnp.zeros_like(st_s)
    gain = gain_ref[...]

    def chunk_rows(ci):
        return pl.ds(pl.multiple_of(ci * CHUNK, CHUNK), CHUNK)

    def advance(ci):
        rows = chunk_rows(ci)
        ss = [st_s[hp] for hp in range(hp_n)]
        sbs = [s.astype(BF16) for s in ss]
        vbs = [(u_s[hp, rows, :] - _dot(w_s[hp, rows, :].astype(BF16), sbs[hp])).astype(BF16) for hp in range(hp_n)]
        for hp in range(hp_n):
            st_s[hp] = ss[hp] * gend_s[hp, ci][0:1, :] + _dot_tn(kd_s[hp, rows, :].astype(BF16), vbs[hp])
            sb_s[hp] = sbs[hp]
            vb_s[hp] = vbs[hp]

    def emit(ci):
        rows = chunk_rows(ci)
        for hp in range(hp_n):
            o = _dot(qe_s[hp, rows, :].astype(BF16), sb_s[hp]) + _dot(attn_s[hp, ci].astype(BF16), vb_s[hp])
            o_ref[rows, hp * LANES:(hp + 1) * LANES] = _gated_head_norm(o, gt_ref[hp, rows, :], gain).astype(o_ref.dtype)

    advance(0)

    def phase2(ci, carry):
        emit(ci - 1)
        advance(ci)
        return carry

    lax.fori_loop(1, n_chunks, phase2, 0)
    emit(n_chunks - 1)
    s_ref[...] = st_s[...]


def _dn_prompt(proj, sm, cw, alog_vec, dtb_vec, gain, *, batch, n_heads, slab_q, slab_g):
    seq = proj.shape[2]
    n_chunks = seq // CHUNK
    hp = math.gcd(DN_HEADS_PER_STEP, n_heads)
    assert slab_q % hp == 0 and slab_g % hp == 0 and n_chunks % DN_CHUNKS_PER_ITER == 0

    def slab_spec(first):
        return pl.BlockSpec((hp, None, seq, LANES), lambda b, j: (first // hp + j, b, 0, 0))

    def cw_spec(first):
        return pl.BlockSpec((hp, DN_CONV, LANES), lambda b, j: (first // hp + j, 0, 0))

    vec_spec = pl.BlockSpec((1, LANES), lambda b, j: (0, 0))
    return pl.pallas_call(
        functools.partial(_dn_prompt_kernel, n_heads=n_heads),
        out_shape=(jax.ShapeDtypeStruct((batch, seq, n_heads * LANES), BF16),
                   jax.ShapeDtypeStruct((batch, n_heads, HEAD_DIM, HEAD_DIM), F32)),
        grid=(batch, n_heads // hp),
        in_specs=[
            slab_spec(slab_q), slab_spec(slab_q + n_heads), slab_spec(slab_q + 2 * n_heads), slab_spec(slab_g),
            pl.BlockSpec((None, seq, LANES), lambda b, j: (b, 0, 0)),
            cw_spec(0), cw_spec(n_heads), cw_spec(2 * n_heads),
            vec_spec, vec_spec, vec_spec,
        ],
        out_specs=(pl.BlockSpec((None, seq, hp * LANES), lambda b, j: (b, 0, j)),
                   pl.BlockSpec((None, hp, HEAD_DIM, HEAD_DIM), lambda b, j: (b, j, 0, 0))),
        scratch_shapes=[pltpu.VMEM((hp, seq, LANES), F32)] * 4 + [
            pltpu.VMEM((hp, n_chunks, CHUNK, CHUNK), F32),
            pltpu.VMEM((hp, n_chunks, SUBLANES, LANES), F32),
            pltpu.VMEM((hp, HEAD_DIM, HEAD_DIM), F32),
            pltpu.VMEM((hp, HEAD_DIM, HEAD_DIM), BF16),
            pltpu.VMEM((hp, CHUNK, HEAD_DIM), BF16),
        ],
        compiler_params=_cparams(("parallel", "parallel")),
        name="dn_prompt",
    )(proj, proj, proj, proj, sm, cw, cw, cw, alog_vec, dtb_vec, gain)


def _hg_lower_bound(logits, layer):
    mx = jnp.max(logits, axis=0, keepdims=True)
    e = jnp.exp(logits - mx)
    p = e / jnp.sum(e, axis=0, keepdims=True)
    lb = jnp.zeros((1, logits.shape[1]), F32)
    for i in range(1, layer + 1):
        lb = lb + p[i:i + 1]
    return lb


def _hg_gates(zf, lb):
    e = jnp.exp(-jnp.abs(zf))
    log_sig = jnp.minimum(zf, 0.0) - jnp.log1p(e)
    log_f = _logaddexp(jnp.log(jnp.maximum(lb, LB_FLOOR)), jnp.log1p(-lb) + log_sig)
    k = (1.0 - lb) * (jnp.where(zf >= 0.0, e, 1.0) / (1.0 + e))
    return log_f, k


def _gla_intra_off(q, k, big_g):
    off = []
    for lo in range(SUB, q.shape[0], SUB):
        ref = big_g[lo - 1:lo]
        qt = q[lo:lo + SUB] * jnp.exp(big_g[lo:lo + SUB] - ref)
        kt = k[:lo] * jnp.exp(ref - big_g[:lo])
        off.append(_dot3(qt, kt, _dot_nt))
    return off


def _gla_intra_diag(q, k, big_g):
    n = q.shape[0]
    lane = lax.broadcasted_iota(jnp.int32, (SUB, n), 1)
    diag = []
    for lo in range(0, n, SUB):
        qi = q[lo:lo + SUB]
        gi = big_g[lo:lo + SUB]
        ki = k[lo:lo + SUB]
        blk = jnp.zeros((SUB, n), F32)
        for j in range(SUB):
            dec = jnp.exp(jnp.minimum(gi - gi[j:j + 1], 0.0))
            col = jnp.sum(qi * ki[j:j + 1] * dec, axis=1, keepdims=True)
            blk = jnp.where(lane == lo + j, col, blk)
        diag.append(blk)
    return diag


def _gla_intra_merge(off, diag):
    n = diag[0].shape[1]
    rows = [diag[0]]
    for bi, o in enumerate(off, start=1):
        rows.append(diag[bi] + jnp.concatenate([o, jnp.zeros((SUB, n - bi * SUB), F32)], axis=1))
    r, c = _tri_incl(n)
    return jnp.where(r >= c, jnp.concatenate(rows, axis=0), 0.0)


GLA_CHUNKS_PER_ITER = 8


def _gla_prompt_kernel(*refs, mode, layer):
    if mode == "gla":
        (q_ref, k_ref, v_ref, gt_ref, sm_ref, wg_ref, bias_ref, gain_ref, o_ref, s_ref,
         st_s, a_s, qe_s, kd_s, de_s) = refs
    else:
        (q_ref, k_ref, v_ref, gt_ref, lbl_ref, gain_ref, o_ref, s_ref, st_s, a_s, qe_s, kd_s, de_s) = refs
        lb = _hg_lower_bound(lbl_ref[...], layer)
    seq = q_ref.shape[0]
    n_groups = seq // (CHUNK * GLA_CHUNKS_PER_ITER)
    r, c = _tri_incl(CHUNK)
    tri = (r >= c).astype(BF16)
    gain = gain_ref[...]
    st_s[...] = jnp.zeros_like(st_s)

    def group_rows(it):
        return [pl.ds(pl.multiple_of((it * GLA_CHUNKS_PER_ITER + u) * CHUNK, CHUNK), CHUNK)
                for u in range(GLA_CHUNKS_PER_ITER)]

    def front(it):
        all_rows, pre = group_rows(it), []
        for rows in all_rows:
            if mode == "gla":
                q = q_ref[rows, :] * (HEAD_DIM ** -0.5)
                k = k_ref[rows, :]
                z = _dot(sm_ref[rows, :].astype(BF16), wg_ref[...]) + bias_ref[...]
                g = _log_sigmoid(z) * (1.0 / GLA_NORMALIZER)
            else:
                q = _silu(q_ref[rows, :])
                g, k = _hg_gates(k_ref[rows, :], lb)
            pre.append((q, k, g))
        big_gs = [_cumsum_rows(tri, g) for _, _, g in pre]
        offs = [_gla_intra_off(q, k, big_g) for (q, k, _), big_g in zip(pre, big_gs)]
        diags = [_gla_intra_diag(q, k, big_g) for (q, k, _), big_g in zip(pre, big_gs)]
        for u, ((q, k, _), big_g, off, diag) in enumerate(zip(pre, big_gs, offs, diags)):
            g_end = big_g[CHUNK - 1:CHUNK, :]
            a_s[u] = _gla_intra_merge(off, diag).astype(BF16)
            qe_s[u] = (q * jnp.exp(big_g)).astype(BF16)
            kd_s[u] = (k * jnp.exp(g_end - big_g)).astype(BF16)
            de_s[u] = jnp.broadcast_to(jnp.exp(g_end), (SUBLANES, LANES))

    def back(it):
        all_rows = group_rows(it)
        vbs = [v_ref[rows, :].astype(BF16) for rows in all_rows]
        intra = [_dot(a_s[u], vb) for u, vb in enumerate(vbs)]
        incr = [_dot_tn(vb, kd_s[u]) for u, vb in enumerate(vbs)]
        states = [st_s[...]]
        for u, t in enumerate(incr):
            states.append(states[-1] * de_s[u][0:1, :] + t)
        for u, (rows, st, av) in enumerate(zip(all_rows, states, intra)):
            o = _dot_nt(qe_s[u], st.astype(BF16)) + av
            o_ref[rows, :] = _gated_head_norm(o, gt_ref[rows, :], gain).astype(o_ref.dtype)
        st_s[...] = states[-1]

    front(0)

    def body(it, carry):
        back(it - 1)
        front(it)
        return carry

    lax.fori_loop(1, n_groups, body, 0)
    back(n_groups - 1)
    s_ref[...] = st_s[...].T


def _gla_prompt(proj, *, mode, layer, batch, n_heads, slabs, sm=None, wg=None, bias=None, lbl=None, gain=None):
    seq = proj.shape[2]

    def slab_spec(first):
        return pl.BlockSpec((None, None, seq, LANES), lambda b, h: (first + h, b, 0, 0))

    in_specs = [slab_spec(s) for s in slabs]
    args = [proj] * 4
    if mode == "gla":
        in_specs += [
            pl.BlockSpec((None, seq, LANES), lambda b, h: (b, 0, 0)),
            pl.BlockSpec((None, LANES, LANES), lambda b, h: (h, 0, 0)),
            pl.BlockSpec((None, 1, LANES), lambda b, h: (h, 0, 0)),
        ]
        args += [sm, wg, bias]
    else:
        in_specs += [pl.BlockSpec((None, lbl.shape[1], LANES), lambda b, h: (h, 0, 0))]
        args += [lbl]
    in_specs.append(pl.BlockSpec((1, LANES), lambda b, h: (0, 0)))
    args.append(gain)
    return pl.pallas_call(
        functools.partial(_gla_prompt_kernel, mode=mode, layer=layer),
        out_shape=(jax.ShapeDtypeStruct((batch, seq, n_heads * LANES), BF16),
                   jax.ShapeDtypeStruct((batch, n_heads, HEAD_DIM, HEAD_DIM), F32)),
        grid=(batch, n_heads),
        in_specs=in_specs,
        out_specs=(pl.BlockSpec((None, seq, LANES), lambda b, h: (b, 0, h)),
                   pl.BlockSpec((None, None, HEAD_DIM, HEAD_DIM), lambda b, h: (b, h, 0, 0))),
        scratch_shapes=[
            pltpu.VMEM((HEAD_DIM, HEAD_DIM), F32),
            pltpu.VMEM((GLA_CHUNKS_PER_ITER, CHUNK, CHUNK), BF16),
            pltpu.VMEM((GLA_CHUNKS_PER_ITER, CHUNK, HEAD_DIM), BF16),
            pltpu.VMEM((GLA_CHUNKS_PER_ITER, CHUNK, HEAD_DIM), BF16),
            pltpu.VMEM((GLA_CHUNKS_PER_ITER, SUBLANES, LANES), F32),
        ],
        compiler_params=_cparams(("parallel", "parallel")),
        name=mode + "_prompt",
    )(*args)


def _diag_extract(row, n_rows, offset):
    r = lax.broadcasted_iota(jnp.int32, (n_rows, LANES), 0)
    l = lax.broadcasted_iota(jnp.int32, (n_rows, LANES), 1)
    return jnp.sum(jnp.where(l == r + offset, jnp.broadcast_to(row, (n_rows, LANES)), 0.0), axis=1, keepdims=True)


def _head_selector(n_heads):
    pad = -(-n_heads // BF16_SUBLANES) * BF16_SUBLANES
    r = lax.broadcasted_iota(jnp.int32, (pad, n_heads * LANES), 0)
    c = lax.broadcasted_iota(jnp.int32, (pad, n_heads * LANES), 1)
    one = (r == lax.shift_right_logical(c, int(math.log2(LANES)))).astype(BF16)
    return jnp.concatenate([one, one, one], axis=0)


def _head_columns(x, sel):
    pad = sel.shape[0] // 3
    if pad > x.shape[0]:
        x = jnp.concatenate([x, jnp.zeros((pad - x.shape[0], x.shape[1]), F32)], axis=0)
    return _dot_tn(jnp.concatenate(_split3(x), axis=0), sel)


def _dn_decode_kernel(p_ref, sm_ref, conv_ref, s_ref, cw_ref, alog_ref, dtb_ref, gain_ref, *rest,
                      n_heads, slab_g, bb):
    o_ref, conv_out_ref, s_out_ref = rest[-3:]
    nh = n_heads
    sel = _head_selector(nh)
    cw = cw_ref[...]
    neg_a = -jnp.exp(alog_ref[...])
    dtb = dtb_ref[...]
    gain = gain_ref[...]
    for b in range(bb):
        x = p_ref[b, 0:3 * nh, :]
        y = cw[DN_CONV - 1] * x
        for i in range(DN_CONV - 1):
            y = y + cw[i] * conv_ref[b, i]
        for i in range(DN_CONV - 2):
            conv_out_ref[b, i] = conv_ref[b, i + 1]
        conv_out_ref[b, DN_CONV - 2] = x
        act = _silu(y)
        q = _l2norm(act[0:nh]) * (HEAD_DIM ** -0.5)
        k = _l2norm(act[nh:2 * nh])
        v = act[2 * nh:3 * nh]
        srow = sm_ref[b:b + 1, :]
        beta = _diag_extract(_sigmoid(srow), nh, 0)
        g = _diag_extract(neg_a * _softplus(srow + dtb), nh, nh)
        eg = jnp.exp(g)
        k_c = _head_columns(k, sel)
        q_c = _head_columns(q, sel)
        w_c = _head_columns(k * (beta * eg), sel)
        u = v * beta
        outs = []
        for h in range(nh):
            cols = slice(h * LANES, (h + 1) * LANES)
            s = s_ref[b, h]
            v_new = u[h:h + 1] - jnp.sum(w_c[:, cols] * s, axis=0, keepdims=True)
            s_new = s * eg[h:h + 1] + k_c[:, cols] * v_new
            s_out_ref[b, h] = s_new
            outs.append(jnp.sum(q_c[:, cols] * s_new, axis=0, keepdims=True))
        o = jnp.concatenate(outs, axis=0)
        gate = p_ref[b, slab_g:slab_g + nh, :]
        o_ref[b] = _gated_head_norm(o, gate, gain).astype(o_ref.dtype)


def _stacked_state_out(s_state, s_blk, layer, s_prev, n_inputs, out_index):
    shape = jax.ShapeDtypeStruct(s_state.shape, F32)
    spec = pl.BlockSpec((None,) + s_blk, lambda i: (layer, i, 0, 0, 0))
    if s_prev is None:
        return shape, spec, [], [], {}
    return shape, spec, [pl.BlockSpec(memory_space=pl.ANY)], [s_prev], {n_inputs: out_index}


def _dn_decode(proj, sm, conv_state, s_state, cw, alog_vec, dtb_vec, gain, *, layer, n_heads, slab_g, bb, s_prev):
    batch, n_slabs, _ = proj.shape
    conv_blk = (bb,) + conv_state.shape[2:]
    s_blk = (bb,) + s_state.shape[2:]
    vec_spec = pl.BlockSpec((1, LANES), lambda i: (0, 0))
    args = [proj, sm, conv_state, s_state, cw, alog_vec, dtb_vec, gain]
    s_shape, s_spec, extra_specs, extra_args, aliases = _stacked_state_out(s_state, s_blk, layer, s_prev, len(args), 2)
    return pl.pallas_call(
        functools.partial(_dn_decode_kernel, n_heads=n_heads, slab_g=slab_g, bb=bb),
        out_shape=(jax.ShapeDtypeStruct((batch, n_heads, LANES), BF16),
                   jax.ShapeDtypeStruct(conv_state.shape[1:], F32),
                   s_shape),
        grid=(batch // bb,),
        in_specs=[
            pl.BlockSpec((bb, n_slabs, LANES), lambda i: (i, 0, 0)),
            pl.BlockSpec((bb, LANES), lambda i: (i, 0)),
            pl.BlockSpec((None,) + conv_blk, lambda i: (layer, i, 0, 0, 0)),
            pl.BlockSpec((None,) + s_blk, lambda i: (layer, i, 0, 0, 0)),
            pl.BlockSpec(cw.shape, lambda i: (0, 0, 0)),
            vec_spec, vec_spec, vec_spec,
        ] + extra_specs,
        out_specs=(pl.BlockSpec((bb, n_heads, LANES), lambda i: (i, 0, 0)),
                   pl.BlockSpec(conv_blk, lambda i: (i, 0, 0, 0)),
                   s_spec),
        input_output_aliases=aliases,
        compiler_params=_cparams(("parallel",)),
        name="dn_decode",
    )(*args, *extra_args)


def _gla_gate_decode_kernel(sm_ref, wg_ref, bias_ref, o_ref):
    z = _dot(sm_ref[...].astype(BF16), wg_ref[...]) + bias_ref[...]
    o_ref[...] = _log_sigmoid(z) * (1.0 / GLA_NORMALIZER)


def _gla_gate_decode(sm, wg_full, bias):
    m = sm.shape[0]
    n = wg_full.shape[1]
    return pl.pallas_call(
        _gla_gate_decode_kernel,
        out_shape=jax.ShapeDtypeStruct((m, n), F32),
        name="gla_gate_decode",
    )(sm, wg_full, bias)


def _gla_decode_kernel(*refs, mode, layer, n_heads, slabs, bb):
    o_ref, s_out_ref = refs[-2:]
    if mode == "gla":
        p_ref, ga_ref, s_ref, gain_ref = refs[:4]
    else:
        p_ref, lbl_ref, s_ref, gain_ref = refs[:4]
    nh = n_heads
    sel = _head_selector(nh)
    sq, sk, sv, sg = slabs
    gain = gain_ref[...]
    if mode == "hg":
        lb = jnp.concatenate([_hg_lower_bound(lbl_ref[h], layer) for h in range(nh)], axis=0)
    for b in range(bb):
        if mode == "gla":
            q = p_ref[b, sq:sq + nh, :] * (HEAD_DIM ** -0.5)
            k = p_ref[b, sk:sk + nh, :]
            g = ga_ref[b]
        else:
            q = _silu(p_ref[b, sq:sq + nh, :])
            g, k = _hg_gates(p_ref[b, sk:sk + nh, :], lb)
        v = p_ref[b, sv:sv + nh, :]
        a_c = _head_columns(jnp.exp(g), sel)
        k_c = _head_columns(k, sel)
        q_c = _head_columns(q, sel)
        outs = []
        for h in range(nh):
            cols = slice(h * LANES, (h + 1) * LANES)
            s_new = a_c[:, cols] * s_ref[b, h] + k_c[:, cols] * v[h:h + 1]
            s_out_ref[b, h] = s_new
            outs.append(jnp.sum(q_c[:, cols] * s_new, axis=0, keepdims=True))
        o = jnp.concatenate(outs, axis=0)
        o_ref[b] = _gated_head_norm(o, p_ref[b, sg:sg + nh, :], gain).astype(o_ref.dtype)


def _gla_decode(proj, s_state, gain, *, mode, layer, n_heads, slabs, bb, s_prev, ga=None, lbl=None):
    batch, n_slabs, _ = proj.shape
    s_blk = (bb,) + s_state.shape[2:]
    if mode == "gla":
        extra, extra_spec = ga, pl.BlockSpec((bb, n_heads, LANES), lambda i: (i, 0, 0))
    else:
        extra, extra_spec = lbl, pl.BlockSpec(lbl.shape, lambda i: (0, 0, 0))
    args = [proj, extra, s_state, gain]
    s_shape, s_spec, extra_specs, extra_args, aliases = _stacked_state_out(s_state, s_blk, layer, s_prev, len(args), 1)
    return pl.pallas_call(
        functools.partial(_gla_decode_kernel, mode=mode, layer=layer, n_heads=n_heads, slabs=slabs, bb=bb),
        out_shape=(jax.ShapeDtypeStruct((batch, n_heads, LANES), BF16), s_shape),
        grid=(batch // bb,),
        in_specs=[
            pl.BlockSpec((bb, n_slabs, LANES), lambda i: (i, 0, 0)),
            extra_spec,
            pl.BlockSpec((None,) + s_blk, lambda i: (layer, i, 0, 0, 0)),
            pl.BlockSpec((1, LANES), lambda i: (0, 0)),
        ] + extra_specs,
        out_specs=(pl.BlockSpec((bb, n_heads, LANES), lambda i: (i, 0, 0)), s_spec),
        input_output_aliases=aliases,
        compiler_params=_cparams(("parallel",)),
        name=mode + "_decode",
    )(*args, *extra_args)


def _in_proj_layout(w_in, dims):
    dn_h, gla_h, hg_h = dims["dn_h"], dims["gla_h"], dims["hg_h"]
    dn_dim, gla_dim, hg_dim = dn_h * HEAD_DIM, gla_h * HEAD_DIM, hg_h * HEAD_DIM
    sizes = (3 * dn_dim, dn_h, dn_h, dn_dim, gla_dim, gla_dim, gla_dim, GLA_LOWRANK, gla_dim,
             hg_dim, hg_dim, hg_dim, hg_dim)
    offs = [0]
    for s in sizes:
        offs.append(offs[-1] + s)
    wide = ((offs[0], offs[1]), (offs[3], offs[7]), (offs[8], offs[13]))
    narrow = ((offs[1], offs[3]), (offs[7], offs[8]))
    depth, d, d_in = w_in.shape
    wt = jnp.swapaxes(w_in, 1, 2)
    used = sum(b - a for a, b in narrow)
    small = jnp.concatenate([wt[:, a:b] for a, b in narrow], axis=1)
    small = jnp.pad(small, ((0, 0), (0, LANES - used), (0, 0)))
    return wt.reshape(depth * d_in, d), wide, small


def _in_proj_kernel(x_ref, w_ref, o_ref, *, slab):
    acc = _dot_nt(x_ref[...], _as_bf16(w_ref[...]))
    if slab:
        for j in range(o_ref.shape[0]):
            o_ref[j] = acc[:, j * LANES:(j + 1) * LANES]
    else:
        o_ref[...] = acc


def _in_proj(h, wt, wide, *, layer, d_in, slab, tm, tn, name):
    m, d = h.shape
    n = sum(b - a for a, b in wide)
    tm = min(tm, m)
    starts, pos = [], 0
    for a, b in wide:
        assert (b - a) % tn == 0 and a % SUBLANES == 0
        starts.append((pos, a - pos))
        pos += b - a

    def w_index(i, j):
        col = j * tn
        shift = starts[0][1]
        for first, sh in starts[1:]:
            shift = jnp.where(col >= first, sh, shift)
        return (pl.multiple_of(layer * d_in + col + shift, SUBLANES), 0)

    if slab:
        out_shape = jax.ShapeDtypeStruct((n // LANES, m, LANES), F32)
        out_spec = pl.BlockSpec((tn // LANES, tm, LANES), lambda i, j: (j, i, 0))
    else:
        out_shape = jax.ShapeDtypeStruct((m, n), F32)
        out_spec = pl.BlockSpec((tm, tn), lambda i, j: (i, j))
    return pl.pallas_call(
        functools.partial(_in_proj_kernel, slab=slab),
        out_shape=out_shape,
        grid=(m // tm, n // tn),
        in_specs=[pl.BlockSpec((tm, d), lambda i, j: (i, 0), pipeline_mode=pl.Buffered(1)),
                  pl.BlockSpec((pl.Element(tn), pl.Element(d)), w_index)],
        out_specs=out_spec,
        compiler_params=_cparams(("parallel", "arbitrary")),
        name=name,
    )(h, wt)


def _layer_weights(l, dn_conv_w, dn_a_log, dn_dt_bias, gla_w_gate, gla_gate_bias, dims):
    dn_h, gla_h = dims["dn_h"], dims["gla_h"]
    gla_dim = gla_h * HEAD_DIM
    lr0 = 2 * dn_h
    wg_rows = jnp.pad(gla_w_gate[l], ((lr0, LANES - lr0 - GLA_LOWRANK), (0, 0))).astype(BF16)
    return dict(
        dn_cw=dn_conv_w[l].reshape(DN_CONV, 3 * dn_h, HEAD_DIM),
        alog_vec=jnp.pad(dn_a_log[l], (dn_h, LANES - 2 * dn_h)).reshape(1, LANES),
        dtb_vec=jnp.pad(dn_dt_bias[l], (dn_h, LANES - 2 * dn_h)).reshape(1, LANES),
        gla_wg_full=wg_rows,
        gla_wg_heads=wg_rows.reshape(LANES, gla_h, HEAD_DIM).transpose(1, 0, 2),
        gla_bias=gla_gate_bias[l].reshape(1, gla_dim),
    )


def _slab_table(dims):
    dn_h, gla_h, hg_h = dims["dn_h"], dims["gla_h"], dims["hg_h"]
    names = [("dn_qkv", 3 * dn_h), ("dn_g", dn_h), ("gla_q", gla_h), ("gla_k", gla_h), ("gla_v", gla_h),
             ("gla_g", gla_h), ("hg_q", hg_h), ("hg_f", hg_h), ("hg_i", hg_h), ("hg_g", hg_h)]
    table, pos = {}, 0
    for n, cnt in names:
        table[n] = pos
        pos += cnt
    return table


def _prompt_layer(x, l, lw, sw, ffn_conv_w, norm_mix, dn_norm, gla_norm, hg_norm, hg_lbl, norm_ffn, dims, batch):
    m, d = x.shape
    seq = m // batch
    dn_h, gla_h, hg_h = dims["dn_h"], dims["gla_h"], dims["hg_h"]
    sl = _slab_table(dims)
    h = _rmsnorm(x, norm_mix[l], BF16)
    proj = _in_proj(h, sw["wt"], sw["wide"], layer=l, d_in=sw["d_in"], slab=True, tm=2048, tn=MXU_DIM, name="in_proj")
    sm = _matmul([h], sw["w_small"], layer=l, w_transposed=True, tm=1024, tn=LANES, name="in_proj_small")
    proj4 = proj.reshape(proj.shape[0], batch, seq, LANES)
    sm3 = sm.reshape(batch, seq, LANES)
    cw = lw["dn_cw"].transpose(1, 0, 2)
    o_dn, s_dn = _dn_prompt(proj4, sm3, cw, lw["alog_vec"], lw["dtb_vec"], dn_norm[l].reshape(1, LANES),
                            batch=batch, n_heads=dn_h, slab_q=sl["dn_qkv"], slab_g=sl["dn_g"])
    o_gla, s_gla = _gla_prompt(proj4, mode="gla", layer=l, batch=batch, n_heads=gla_h,
                               slabs=(sl["gla_q"], sl["gla_k"], sl["gla_v"], sl["gla_g"]),
                               sm=sm3, wg=lw["gla_wg_heads"], bias=lw["gla_bias"].reshape(gla_h, 1, LANES),
                               gain=gla_norm[l].reshape(1, LANES))
    o_hg, s_hg = _gla_prompt(proj4, mode="hg", layer=l, batch=batch, n_heads=hg_h,
                             slabs=(sl["hg_q"], sl["hg_f"], sl["hg_i"], sl["hg_g"]),
                             lbl=hg_lbl, gain=hg_norm[l].reshape(1, LANES))
    o_list = [o_dn.reshape(m, -1), o_gla.reshape(m, -1), o_hg.reshape(m, -1)]
    x = _matmul(o_list, sw["wo"], layer=l, res=x, tm=1024, tn=512, name="out_proj")
    h2 = _rmsnorm(x, norm_ffn[l], BF16)
    hidden, tail = _ffn_up_prompt(h2, sw["ffn_wg"], sw["ffn_wu"], ffn_conv_w, sw["ffn_cb"],
                                  layer=l, batch=batch, tn=MXU_DIM)
    x = _matmul([hidden], sw["ffn_wd"], layer=l, res=x, tm=512, tn=512, name="ffn_down")
    n_qkv = 3 * dn_h
    conv_dn = proj4[sl["dn_qkv"]:sl["dn_qkv"] + n_qkv, :, seq - (DN_CONV - 1):, :]
    conv_dn = conv_dn.transpose(1, 2, 0, 3).reshape(batch, DN_CONV - 1, n_qkv * LANES)
    conv_ffn = tail[:, SUBLANES - (FFN_CONV - 1):, :]
    return x, (conv_dn, s_dn, s_gla, s_hg, conv_ffn)


def _decode_layer(x, l, lw, sw, ffn_conv_w, st_dn_conv, st_dn, st_gla, st_hg, st_ffn, norm_mix, dn_norm, gla_norm,
                  hg_norm, hg_lbl, norm_ffn, dims, prev):
    m, d = x.shape
    depth = st_dn.shape[0]
    dn_h, gla_h, hg_h = dims["dn_h"], dims["gla_h"], dims["hg_h"]
    sl = _slab_table(dims)
    bb = SUBLANES
    h = _rmsnorm(x, norm_mix[l], BF16)
    proj = _in_proj(h, sw["wt"], sw["wide"], layer=l, d_in=sw["d_in"], slab=False, tm=m, tn=MXU_DIM,
                    name="in_proj_dec")
    sm = _matmul([h], sw["w_small"], layer=l, w_transposed=True, tm=m, tn=LANES, name="in_proj_small_dec")
    proj3 = proj.reshape(m, -1, LANES)
    conv_in = st_dn_conv.reshape(depth, m, DN_CONV - 1, 3 * dn_h, LANES)
    o_dn, conv_dn, s_dn = _dn_decode(proj3, sm, conv_in, st_dn, lw["dn_cw"], lw["alog_vec"], lw["dtb_vec"],
                                     dn_norm[l].reshape(1, LANES), layer=l, n_heads=dn_h, slab_g=sl["dn_g"], bb=bb,
                                     s_prev=prev[0])
    ga = _gla_gate_decode(sm, lw["gla_wg_full"], lw["gla_bias"]).reshape(m, gla_h, LANES)
    o_gla, s_gla = _gla_decode(proj3, st_gla, gla_norm[l].reshape(1, LANES), mode="gla", layer=l, n_heads=gla_h,
                               slabs=(sl["gla_q"], sl["gla_k"], sl["gla_v"], sl["gla_g"]), bb=bb, s_prev=prev[1],
                               ga=ga)
    o_hg, s_hg = _gla_decode(proj3, st_hg, hg_norm[l].reshape(1, LANES), mode="hg", layer=l, n_heads=hg_h,
                             slabs=(sl["hg_q"], sl["hg_f"], sl["hg_i"], sl["hg_g"]), bb=bb, s_prev=prev[2],
                             lbl=hg_lbl)
    o_list = [o_dn.reshape(m, -1), o_gla.reshape(m, -1), o_hg.reshape(m, -1)]
    x = _matmul(o_list, sw["wo"], layer=l, res=x, tm=m, tn=1024, name="out_proj_dec")
    h2 = _rmsnorm(x, norm_ffn[l], BF16)
    st_flat = st_ffn.reshape(depth, m, -1)
    hidden, gate = _ffn_up_decode(h2, sw["ffn_wg"], sw["ffn_wu"], st_flat, ffn_conv_w, sw["ffn_cb"],
                                  layer=l, tn=MXU_DIM)
    x = _matmul([hidden], sw["ffn_wd"], layer=l, res=x, tm=m, tn=512, name="ffn_down_dec")
    conv_ffn = jnp.stack([st_ffn[l][:, 1, :], gate], axis=1)
    return x, (conv_dn.reshape(m, DN_CONV - 1, -1), conv_ffn), (s_dn, s_gla, s_hg)


def kernel(x_prompt, x_sample, state_dn_conv, state_dn, state_gla, state_hg, state_ffn_conv, norm_mix, w_in, dn_conv_w, dn_a_log, dn_dt_bias, dn_norm, gla_w_gate, gla_gate_bias, gla_norm, hg_lb_logits, hg_norm, w_out, norm_ffn, ffn_w_gate, ffn_w_up, ffn_conv_w, ffn_conv_b, ffn_w_down, norm_final):
    batch, seq, d_model = x_prompt.shape
    dec_batch, dec_seq, _ = x_sample.shape
    assert dec_seq == 1 and FFN_CONV == 3
    depth = w_in.shape[0]
    dims = dict(dn_h=state_dn.shape[2], gla_h=state_gla.shape[2], hg_h=state_hg.shape[2])
    hg_lbl = hg_lb_logits.reshape(depth, dims["hg_h"], HEAD_DIM).transpose(1, 0, 2)

    xp = x_prompt.reshape(batch * seq, d_model)
    xs = x_sample.reshape(dec_batch, d_model)
    p_states, s_states = [], []
    wt, wide, w_small = _in_proj_layout(w_in, dims)
    sw = dict(wt=wt, wide=wide, d_in=w_in.shape[2], w_small=w_small, wo=w_out, ffn_wg=ffn_w_gate, ffn_wu=ffn_w_up,
              ffn_wd=ffn_w_down.astype(BF16), ffn_cb=ffn_conv_b.reshape(depth, 1, -1))
    s_mats = (None, None, None)
    for l in range(depth):
        lw = _layer_weights(l, dn_conv_w, dn_a_log, dn_dt_bias, gla_w_gate, gla_gate_bias, dims)
        xp, ps = _prompt_layer(xp, l, lw, sw, ffn_conv_w, norm_mix, dn_norm, gla_norm, hg_norm, hg_lbl, norm_ffn,
                               dims, batch)
        xs, ss, s_mats = _decode_layer(xs, l, lw, sw, ffn_conv_w, state_dn_conv, state_dn, state_gla, state_hg,
                                       state_ffn_conv, norm_mix, dn_norm, gla_norm, hg_norm, hg_lbl, norm_ffn, dims,
                                       s_mats)
        p_states.append(ps)
        s_states.append(ss)
    y_prompt = _rmsnorm(xp, norm_final, F32).reshape(batch, seq, d_model)
    y_sample = _rmsnorm(xs, norm_final, F32).reshape(dec_batch, dec_seq, d_model)
    stack = lambda states, i: jnp.stack([s[i] for s in states], axis=0)
    return ((y_prompt, y_sample) + tuple(stack(p_states, i) for i in range(5))
            + (stack(s_states, 0),) + s_mats + (stack(s_states, 1),))
```
